```python
import math
import jax, jax.numpy as jnp
from jax import lax
import numpy as np

D_MODEL = 1024
BATCH = 16
SEQ = 256
DEPTH = 4
DEC_BATCH = 4
DEC_SEQ = 2048
PAST_LEN = 512

GRID_W = 64
Q_BLOCK = 128
ROPE_THETA = 10000.0
H_A = 8
D_A = 64
H_B = 8
G_B = 4
D_B = 128
N_EXPERTS = 16
N_GROUPS = 4
EXPERTS_PER_GROUP = N_EXPERTS // N_GROUPS
TOP_K = 2
D_EXPERT = 512
ALPHA = (2 * DEPTH) ** 0.25
BETA = (8 * DEPTH) ** -0.25
LN_EPS = 1e-6
RMS_EPS = 1e-6
W_QA = H_A * 2 * D_A
W_KA = H_A * 2 * D_A
W_VA = H_A * 2 * D_A
W_QB = H_B * D_B
W_KB = G_B * D_B
W_VB = G_B * D_B
W_GATE = 2 * D_MODEL
OFF_KA = W_QA
OFF_VA = OFF_KA + W_KA
OFF_QB = OFF_VA + W_VA
OFF_KB = OFF_QB + W_QB
OFF_VB = OFF_KB + W_KB
OFF_GATE = OFF_VB + W_VB
D_IN = OFF_GATE + W_GATE
SPLIT_POINTS = (OFF_KA, OFF_VA, OFF_QB, OFF_KB, OFF_VB, OFF_GATE)

kernel_name = "hybrid_diff_gqa_moe_prefix_diffusion_step"


def layer_norm(x, g, b):
    xf = x.astype(jnp.float32)
    mu = jnp.mean(xf, axis=-1, keepdims=True)
    xc = xf - mu
    var = jnp.mean(xc * xc, axis=-1, keepdims=True)
    return (xc * lax.rsqrt(var + LN_EPS) * g + b).astype(x.dtype)


def rms_norm(x, g):
    xf = x.astype(jnp.float32)
    y = xf * lax.rsqrt(jnp.mean(xf * xf, axis=-1, keepdims=True) + RMS_EPS)
    return (y * g).astype(x.dtype)


def axial_rope(x, row, col):
    d = x.shape[-1]
    nf = d // 4
    inv = ROPE_THETA ** (-jnp.arange(nf, dtype=jnp.float32) / nf)
    ang = jnp.stack([row[:, None] * inv, col[:, None] * inv], axis=-2)
    ang = ang.reshape(ang.shape[0], *([1] * (x.ndim - 3)), 2, nf)
    cos, sin = jnp.cos(ang), jnp.sin(ang)
    xr = x.astype(jnp.float32).reshape(*x.shape[:-1], 2, 2, nf)
    x1, x2 = xr[..., 0, :], xr[..., 1, :]
    out = jnp.stack([x1 * cos - x2 * sin, x2 * cos + x1 * sin], axis=-2)
    return out.reshape(x.shape).astype(x.dtype)


def sweep_query_blocks(attend, q):
    b, t = q.shape[:2]
    qb = jnp.moveaxis(q.reshape(b, t // Q_BLOCK, Q_BLOCK, *q.shape[2:]), 1, 0)
    out = lax.map(attend, qb)
    return jnp.moveaxis(out, 0, 1).reshape(b, t, *out.shape[3:])


def diff_attention(q, k, v, lam):
    scale = D_A ** -0.5

    def attend(qb):
        s = jnp.einsum('bqhmd,bkhmd->mbhqk', qb, k).astype(jnp.float32) * scale
        p = jax.nn.softmax(s, axis=-1)
        a = p[0] - lam * p[1]
        return jnp.einsum('bhqk,bkhe->bqhe', a.astype(v.dtype), v)

    return sweep_query_blocks(attend, q)


def gqa_attention(q, k, v):
    scale = D_B ** -0.5

    def attend(qb):
        bq, nq = qb.shape[:2]
        qg = qb.reshape(bq, nq, G_B, H_B // G_B, D_B)
        s = jnp.einsum('bqgrd,bkgd->bgrqk', qg, k).astype(jnp.float32) * scale
        p = jax.nn.softmax(s, axis=-1)
        o = jnp.einsum('bgrqk,bkgd->bqgrd', p.astype(v.dtype), v)
        return o.reshape(bq, nq, H_B * D_B)

    return sweep_query_blocks(attend, q)


def mix_block(h, pos, ctx, lam_init, w_in, lam_q1, lam_k1, lam_q2, lam_k2, subln_g,
              qn_g, kn_g, w_br_a, w_br_b, w_out):
    b, t, _ = h.shape
    proj = h @ w_in
    qa, ka, va, qb, kb, vb, gates = jnp.split(proj, SPLIT_POINTS, axis=-1)
    qa = qa.reshape(b, t, H_A, 2, D_A)
    ka = ka.reshape(b, t, H_A, 2, D_A)
    va = va.reshape(b, t, H_A, 2 * D_A)
    qb = rms_norm(qb.reshape(b, t, H_B, D_B), qn_g)
    kb = rms_norm(kb.reshape(b, t, G_B, D_B), kn_g)
    vb = vb.reshape(b, t, G_B, D_B)
    new_kv = (ka.reshape(b, t, H_A, 2 * D_A), va, kb, vb)
    if pos is None:
        keys_a, vals_a, keys_b, vals_b = ka, va, kb, vb
    else:
        row, col = pos
        qa, ka = axial_rope(qa, row, col), axial_rope(ka, row, col)
        qb, kb = axial_rope(qb, row, col), axial_rope(kb, row, col)
        ck_a, cv_a, ck_b, cv_b = ctx
        s = ck_a.shape[1]
        keys_a = jnp.concatenate([ck_a.reshape(b, s, H_A, 2, D_A), ka], axis=1)
        vals_a = jnp.concatenate([cv_a, va], axis=1)
        keys_b = jnp.concatenate([ck_b, kb], axis=1)
        vals_b = jnp.concatenate([cv_b, vb], axis=1)
    lam = (jnp.exp(jnp.sum(lam_q1.astype(jnp.float32) * lam_k1.astype(jnp.float32)))
           - jnp.exp(jnp.sum(lam_q2.astype(jnp.float32) * lam_k2.astype(jnp.float32))) + lam_init)
    oa = diff_attention(qa, keys_a, vals_a, lam)
    oa = (rms_norm(oa, subln_g) * (1.0 - lam_init)).reshape(b, t, H_A * 2 * D_A)
    ob = gqa_attention(qb, keys_b, vals_b)
    ga, gb = jnp.split(gates, 2, axis=-1)
    merged = jax.nn.sigmoid(ga) * (oa @ w_br_a) + jax.nn.sigmoid(gb) * (ob @ w_br_b)
    return merged @ w_out, new_kv


def moe(h, w_router, router_bias, w_gate, w_up, w_down):
    b, t, d = h.shape
    tok = h.reshape(b * t, d)
    scores = jax.nn.sigmoid((tok @ w_router).astype(jnp.float32))
    biased = scores + router_bias.astype(jnp.float32)
    grp = biased.reshape(-1, N_GROUPS, EXPERTS_PER_GROUP)
    grp_score = jnp.sum(lax.top_k(grp, TOP_K)[0], axis=-1)
    sel_group = jnp.argmax(grp_score, axis=-1)
    in_group = (jnp.arange(N_EXPERTS) // EXPERTS_PER_GROUP)[None, :] == sel_group[:, None]
    _, idx = lax.top_k(jnp.where(in_group, biased, -jnp.inf), TOP_K)
    w = jnp.take_along_axis(scores, idx, axis=-1)
    w = w / jnp.sum(w, axis=-1, keepdims=True)
    combine = jnp.sum(jax.nn.one_hot(idx, N_EXPERTS, dtype=jnp.float32) * w[..., None], axis=-2)
    g = jnp.einsum('nd,edf->nef', tok, w_gate)
    u = jnp.einsum('nd,edf->nef', tok, w_up)
    act = jax.nn.silu(g) * u * combine[..., None].astype(tok.dtype)
    y = jnp.einsum('nef,efd->nd', act, w_down)
    return y.reshape(b, t, d)


def trunk_layer(x, cond, pos, ctx, lam_init, w_mod, b_mod, w_in, lam_q1, lam_k1, lam_q2, lam_k2,
                subln_g, qn_g, kn_g, w_br_a, w_br_b, w_out, ln1_g, ln1_b, ln2_g, ln2_b,
                w_router, router_bias, w_gate, w_up, w_down):
    sh1, sc1, g1, sh2, sc2, g2 = jnp.split(jax.nn.silu(cond) @ w_mod + b_mod, 6, axis=-1)
    h = x * (1.0 + sc1) + sh1
    mix, new_kv = mix_block(h, pos, ctx, lam_init, w_in, lam_q1, lam_k1, lam_q2, lam_k2, subln_g,
                            qn_g, kn_g, w_br_a, w_br_b, w_out)
    x = layer_norm(ALPHA * x + g1 * mix, ln1_g, ln1_b)
    h = x * (1.0 + sc2) + sh2
    x = layer_norm(ALPHA * x + g2 * moe(h, w_router, router_bias, w_gate, w_up, w_down), ln2_g, ln2_b)
    return x, new_kv


def setup_inputs(seed: int = 0) -> dict:
    key = jax.random.key(seed)
    ks = iter(jax.random.split(key, 40))
    nrm = lambda shape, s=1.0: s * jax.random.normal(next(ks), shape, dtype=jnp.float32)
    gain = lambda shape: 1.0 + nrm(shape, 0.01)
    col_scale = jnp.ones((D_IN,), jnp.float32).at[OFF_VA:OFF_QB].set(BETA).at[OFF_VB:OFF_GATE].set(BETA)
    return {
        "x_prompt": nrm((BATCH, SEQ, D_MODEL)),
        "x_sample": nrm((DEC_BATCH, DEC_SEQ, D_MODEL)),
        "cache_a_k": nrm((DEC_BATCH, DEPTH, PAST_LEN, H_A, 2 * D_A)),
        "cache_a_v": nrm((DEC_BATCH, DEPTH, PAST_LEN, H_A, 2 * D_A), BETA),
        "cache_b_k": nrm((DEC_BATCH, DEPTH, PAST_LEN, G_B, D_B)),
        "cache_b_v": nrm((DEC_BATCH, DEPTH, PAST_LEN, G_B, D_B), BETA),
        "c": nrm((DEC_BATCH, D_MODEL)),
        "c_ctx": nrm((D_MODEL,)),
        "w_mod": nrm((DEPTH, D_MODEL, 6 * D_MODEL), 0.5 * D_MODEL ** -0.5),
        "b_mod": nrm((DEPTH, 6 * D_MODEL), 0.01),
        "w_in": nrm((DEPTH, D_MODEL, D_IN), D_MODEL ** -0.5) * col_scale,
        "lam_q1": nrm((DEPTH, D_A), 0.1),
        "lam_k1": nrm((DEPTH, D_A), 0.1),
        "lam_q2": nrm((DEPTH, D_A), 0.1),
        "lam_k2": nrm((DEPTH, D_A), 0.1),
        "subln_g": gain((DEPTH, 2 * D_A)),
        "qn_g": gain((DEPTH, D_B)),
        "kn_g": gain((DEPTH, D_B)),
        "w_br_a": nrm((DEPTH, H_A * 2 * D_A, D_MODEL), BETA * (H_A * 2 * D_A) ** -0.5),
        "w_br_b": nrm((DEPTH, H_B * D_B, D_MODEL), BETA * (H_B * D_B) ** -0.5),
        "w_out": nrm((DEPTH, D_MODEL, D_MODEL), BETA * D_MODEL ** -0.5),
        "ln1_g": gain((DEPTH, D_MODEL)),
        "ln1_b": nrm((DEPTH, D_MODEL), 0.01),
        "ln2_g": gain((DEPTH, D_MODEL)),
        "ln2_b": nrm((DEPTH, D_MODEL), 0.01),
        "w_router": nrm((D_MODEL, N_EXPERTS), D_MODEL ** -0.5),
        "router_bias": nrm((N_EXPERTS,), 0.01),
        "w_gate": nrm((DEPTH, N_EXPERTS, D_MODEL, D_EXPERT), BETA * D_MODEL ** -0.5),
        "w_up": nrm((DEPTH, N_EXPERTS, D_MODEL, D_EXPERT), BETA * D_MODEL ** -0.5),
        "w_down": nrm((DEPTH, N_EXPERTS, D_EXPERT, D_MODEL), BETA * D_EXPERT ** -0.5),
    }


def reference(x_prompt, x_sample, cache_a_k, cache_a_v, cache_b_k, cache_b_v, c, c_ctx,
              w_mod, b_mod, w_in, lam_q1, lam_k1, lam_q2, lam_k2, subln_g, qn_g, kn_g,
              w_br_a, w_br_b, w_out, ln1_g, ln1_b, ln2_g, ln2_b, w_router, router_bias,
              w_gate, w_up, w_down):
    def layer_weights(l):
        return (w_mod[l], b_mod[l], w_in[l], lam_q1[l], lam_k1[l], lam_q2[l], lam_k2[l], subln_g[l],
                qn_g[l], kn_g[l], w_br_a[l], w_br_b[l], w_out[l], ln1_g[l], ln1_b[l], ln2_g[l], ln2_b[l],
                w_router, router_bias, w_gate[l], w_up[l], w_down[l])

    def lambda_init(l):
        return 0.8 - 0.6 * math.exp(-0.3 * l)

    y_prompt = x_prompt
    cond_ctx = c_ctx[None, None, :]
    ak, av, bk, bv = [], [], [], []
    for l in range(DEPTH):
        y_prompt, kv = trunk_layer(y_prompt, cond_ctx, None, None, lambda_init(l), *layer_weights(l))
        ak.append(kv[0]); av.append(kv[1]); bk.append(kv[2]); bv.append(kv[3])
    state_a_k = jnp.stack(ak, axis=1)
    state_a_v = jnp.stack(av, axis=1)
    state_b_k = jnp.stack(bk, axis=1)
    state_b_v = jnp.stack(bv, axis=1)

    n_tok = x_sample.shape[1]
    n_rows = n_tok // GRID_W
    row = jnp.repeat(jnp.arange(n_rows, dtype=jnp.float32), GRID_W)
    col = jnp.tile(jnp.arange(GRID_W, dtype=jnp.float32), n_rows)
    cond_lat = c[:, None, :]
    y_sample = x_sample
    for l in range(DEPTH):
        ctx = (cache_a_k[:, l], cache_a_v[:, l], cache_b_k[:, l], cache_b_v[:, l])
        y_sample, _ = trunk_layer(y_sample, cond_lat, (row, col), ctx, lambda_init(l), *layer_weights(l))

    return (y_prompt, y_sample, state_a_k, state_a_v, state_b_k, state_b_v)
```

```python
import functools
import math

import jax
import jax.numpy as jnp
from jax import lax
from jax.experimental import pallas as pl
from jax.experimental.pallas import tpu as pltpu

D_MODEL = 1024
DEPTH = 4
GRID_W = 64
ROPE_THETA = 10000.0
H_A = 8
D_A = 64
H_B = 8
G_B = 4
D_B = 128
HEAD_W = 128
N_EXPERTS = 16
EXPERTS_PER_GROUP = 4
D_EXPERT = 512
ALPHA = (2 * DEPTH) ** 0.25
LN_EPS = 1e-6
RMS_EPS = 1e-6
W_QKV = 3 * H_A * 2 * D_A + H_B * D_B + 2 * G_B * D_B
LOG2E = 1.4426950408889634
LANES = 128
VMEM_LIMIT = 56 * 1024 * 1024

BF16 = jnp.bfloat16
F32 = jnp.float32

TM_QKV = 256
TM_POST = 256
TM_MOE = 1024
TN_MOD = 1536


def _params(n_axes, vmem=VMEM_LIMIT):
    return pltpu.CompilerParams(dimension_semantics=("arbitrary",) * n_axes,
                                vmem_limit_bytes=vmem)


def _mod_kernel(c_ref, w_ref, b_ref, o_ref):
    c = c_ref[...]
    s = (c * jax.nn.sigmoid(c)).astype(BF16)
    o_ref[...] = jnp.dot(s, w_ref[...].astype(BF16), preferred_element_type=F32) + b_ref[...]


def _modulation(cond8, w_mod, b_mod):
    n_out = 6 * D_MODEL
    return pl.pallas_call(
        _mod_kernel,
        grid=(DEPTH, n_out // TN_MOD),
        in_specs=[
            pl.BlockSpec((8, D_MODEL), lambda l, j: (0, 0)),
            pl.BlockSpec((None, D_MODEL, TN_MOD), lambda l, j: (l, 0, j)),
            pl.BlockSpec((None, 1, TN_MOD), lambda l, j: (l, 0, j)),
        ],
        out_specs=pl.BlockSpec((None, 8, TN_MOD), lambda l, j: (l, 0, j)),
        out_shape=jax.ShapeDtypeStruct((DEPTH, 8, n_out), F32),
        compiler_params=_params(2),
        name="modulation",
    )(cond8, w_mod, b_mod.reshape(DEPTH, 1, n_out))


def _rope_tables(n_tok, d):
    nf = d // 4
    n_rows = n_tok // GRID_W
    row = jnp.repeat(jnp.arange(n_rows, dtype=F32), GRID_W)
    col = jnp.tile(jnp.arange(GRID_W, dtype=F32), n_rows)
    inv = ROPE_THETA ** (-jnp.arange(nf, dtype=F32) / nf)
    lane = jnp.arange(LANES) % d
    axis = lane // (2 * nf)
    half = (lane % (2 * nf)) // nf
    freq = lane % nf
    pos = jnp.where(axis[None, :] == 0, row[:, None], col[:, None])
    ang = pos * inv[freq][None, :]
    cos, sin = jnp.cos(ang), jnp.sin(ang)
    sin_lo = jnp.where(half[None, :] == 0, -sin, 0.0)
    sin_hi = jnp.where(half[None, :] == 1, sin, 0.0)
    return cos, sin_lo, sin_hi


def _rope(x, cos, sin_lo, sin_hi, nf):
    up = pltpu.roll(x, LANES - nf, 1)
    down = pltpu.roll(x, nf, 1)
    return x * cos + up * sin_lo + down * sin_hi


def _rms(x, g):
    ms = jnp.mean(x * x, axis=-1, keepdims=True)
    return x * lax.rsqrt(ms + RMS_EPS) * g


def _qkv_kernel(*refs, rope, states):
    x_ref, mod_ref, w_ref, qn_ref, kn_ref = refs[:5]
    pos = 5
    if rope:
        ca, sa_lo, sa_hi, cb, sb_lo, sb_hi = (r[...] for r in refs[pos:pos + 6])
        pos += 6
    if states:
        pos += 4
    qa_ref, ka_ref, va_ref, qb_ref, kb_ref, vb_ref = refs[pos:pos + 6]
    pos += 6
    if states:
        sak_ref, sav_ref, sbk_ref, sbv_ref = refs[pos:pos + 4]

    sh1 = mod_ref[:, 0:D_MODEL]
    sc1 = mod_ref[:, D_MODEL:2 * D_MODEL]
    h = (x_ref[...] * (1.0 + sc1) + sh1).astype(BF16)

    def seg(lo, width):
        return jnp.dot(h, w_ref[:, lo:lo + width], preferred_element_type=F32)

    def heads(p, n):
        return [p[:, i * HEAD_W:(i + 1) * HEAD_W] for i in range(n)]

    qa_scale = (D_A ** -0.5) * LOG2E
    qb_scale = (D_B ** -0.5) * LOG2E
    nfa, nfb = D_A // 4, D_B // 4

    p = seg(0, H_A * HEAD_W)
    for i, c in enumerate(heads(p, H_A)):
        if rope:
            c = _rope(c, ca, sa_lo, sa_hi, nfa)
        qa_ref[i] = (c * qa_scale).astype(BF16)

    p = seg(H_A * HEAD_W, H_A * HEAD_W)
    if states:
        sak_ref[...] = p
    for i, c in enumerate(heads(p, H_A)):
        if rope:
            c = _rope(c, ca, sa_lo, sa_hi, nfa)
        ka_ref[i] = c.astype(BF16)

    p = seg(2 * H_A * HEAD_W, H_A * HEAD_W)
    if states:
        sav_ref[...] = p
    for i, c in enumerate(heads(p, H_A)):
        va_ref[i] = c.astype(BF16)

    off = 3 * H_A * HEAD_W
    p = seg(off, H_B * HEAD_W)
    qn = qn_ref[...]
    for i, c in enumerate(heads(p, H_B)):
        c = _rms(c, qn)
        if rope:
            c = _rope(c, cb, sb_lo, sb_hi, nfb)
        qb_ref[i] = (c * qb_scale).astype(BF16)

    off += H_B * HEAD_W
    p = seg(off, G_B * HEAD_W)
    kn = kn_ref[...]
    for i, c in enumerate(heads(p, G_B)):
        c = _rms(c, kn)
        if states:
            sbk_ref[:, i * HEAD_W:(i + 1) * HEAD_W] = c
        if rope:
            c = _rope(c, cb, sb_lo, sb_hi, nfb)
        kb_ref[i] = c.astype(BF16)

    off += G_B * HEAD_W
    p = seg(off, G_B * HEAD_W)
    if states:
        sbv_ref[...] = p
    for i, c in enumerate(heads(p, G_B)):
        vb_ref[i] = c.astype(BF16)


def _qkv(x, mod_rows, w_in_bf, qn_g, kn_g, layer, *, tok_per_batch, rope_tabs=None, state_bufs=None):
    n = x.shape[0]
    tm = TM_QKV
    tiles_per_batch = tok_per_batch // tm
    rope = rope_tabs is not None
    states = state_bufs is not None
    row0 = 1 if rope else 0

    in_specs = [
        pl.BlockSpec((tm, D_MODEL), lambda i: (i, 0)),
        pl.BlockSpec((None, 1, 6 * D_MODEL), lambda i: (row0 + i // tiles_per_batch if rope else 0, 0, 0)),
        pl.BlockSpec((None, D_MODEL, W_QKV), lambda i: (layer, 0, 0)),
        pl.BlockSpec((1, HEAD_W), lambda i: (0, 0)),
        pl.BlockSpec((1, HEAD_W), lambda i: (0, 0)),
    ]
    args = [x, mod_rows, w_in_bf, qn_g, kn_g]
    if rope:
        in_specs += [pl.BlockSpec((tm, LANES), lambda i: (i % tiles_per_batch, 0))] * 6
        args += list(rope_tabs)
    aliases = {}
    if states:
        for k, buf in enumerate(state_bufs):
            aliases[len(args)] = 6 + k
            in_specs.append(pl.BlockSpec(memory_space=pl.ANY))
            args.append(buf)

    def hm(nh):
        return (pl.BlockSpec((nh, tm, HEAD_W), lambda i: (0, i, 0)),
                jax.ShapeDtypeStruct((nh, n, HEAD_W), BF16))

    outs = [hm(H_A), hm(H_A), hm(H_A), hm(H_B), hm(G_B), hm(G_B)]
    if states:
        assert tm == tok_per_batch
        for buf in state_bufs:
            outs.append((pl.BlockSpec((None, None, tm, buf.shape[-1]), lambda i: (i, layer, 0, 0)),
                         jax.ShapeDtypeStruct(buf.shape, F32)))
    return pl.pallas_call(
        functools.partial(_qkv_kernel, rope=rope, states=states),
        grid=(n // tm,),
        in_specs=in_specs,
        out_specs=[o[0] for o in outs],
        out_shape=[o[1] for o in outs],
        input_output_aliases=aliases,
        compiler_params=_params(1),
        name="qkv_states" if states else "qkv_rope",
    )(*args)


def _attn_kernel(*refs, diff, n_cache, n_new, tq, lam_init):
    if diff:
        lq1, lk1, lq2, lk2, g_ref = refs[:5]
        refs = refs[5:]
    q_ref, kn_ref, vn_ref = refs[:3]
    refs = refs[3:]
    if n_cache:
        kc_ref, vc_ref = refs[:2]
        refs = refs[2:]
    o_ref = refs[0]

    if n_cache:
        k_all, v_all = refs[1:3]

        @pl.when(pl.program_id(2) == 0)
        def _():
            k_all[0:n_cache, :] = kc_ref[...].astype(BF16)
            v_all[0:n_cache, :] = vc_ref[...].astype(BF16)
            k_all[n_cache:n_cache + n_new, :] = kn_ref[...]
            v_all[n_cache:n_cache + n_new, :] = vn_ref[...]

        k = k_all[...]
        v = v_all[...]
    else:
        k = kn_ref[...]
        v = vn_ref[...]

    if diff:
        q = q_ref[...]
        lane = lax.broadcasted_iota(jnp.int32, q.shape, 1)
        zero = jnp.zeros_like(q)
        qs = jnp.concatenate([jnp.where(lane < D_A, q, zero), jnp.where(lane >= D_A, q, zero)], axis=0)
    else:
        qs = q_ref[...].reshape(2 * tq, HEAD_W)

    s = lax.dot_general(qs, k, (((1,), (1,)), ((), ())), preferred_element_type=F32)
    m = jnp.max(s, axis=-1, keepdims=True)
    e = jnp.exp2(s - m)
    den = jnp.sum(e, axis=-1, keepdims=True)

    if diff:
        lam = (jnp.exp(jnp.sum(lq1[...] * lk1[...], axis=-1, keepdims=True))
               - jnp.exp(jnp.sum(lq2[...] * lk2[...], axis=-1, keepdims=True)) + lam_init)
        d1, d2 = den[:tq], den[tq:]
        a = e[:tq] - e[tq:] * (lam * d1 / d2)
        o = jnp.dot(a.astype(BF16), v, preferred_element_type=F32) / d1
        o = _rms(o, g_ref[...]) * (1.0 - lam_init)
        o_ref[...] = o.astype(BF16)
    else:
        o = jnp.dot(e.astype(BF16), v, preferred_element_type=F32) / den
        o_ref[...] = o.astype(BF16).reshape(2, tq, HEAD_W)


def _attention(q, k_new, v_new, *, diff, n_batch, tok_per_batch, tq, layer, lam_init=0.0,
               lam_params=None, subln_g=None, cache_k=None, cache_v=None):
    n_kv = k_new.shape[0]
    n = q.shape[1]
    nq = tok_per_batch // tq
    n_cache = 0 if cache_k is None else cache_k.shape[2]
    qh = 1 if diff else 2

    in_specs, args = [], []
    if diff:
        in_specs += [pl.BlockSpec((1, D_A), lambda b, h, i: (0, 0))] * 4
        args += list(lam_params)
        in_specs.append(pl.BlockSpec((1, HEAD_W), lambda b, h, i: (0, 0)))
        args.append(subln_g)
    q_block = (None, tq, HEAD_W) if diff else (2, tq, HEAD_W)
    in_specs += [
        pl.BlockSpec(q_block, lambda b, h, i: (h, b * nq + i, 0)),
        pl.BlockSpec((None, tok_per_batch, HEAD_W), lambda b, h, i: (h, b, 0)),
        pl.BlockSpec((None, tok_per_batch, HEAD_W), lambda b, h, i: (h, b, 0)),
    ]
    args += [q, k_new, v_new]
    scratch = []
    if n_cache:
        cspec = pl.BlockSpec((None, None, n_cache, HEAD_W), lambda b, h, i: (b, layer, 0, h))
        in_specs += [cspec, cspec]
        args += [cache_k, cache_v]
        scratch = [pltpu.VMEM((n_cache + tok_per_batch, HEAD_W), BF16)] * 2
    return pl.pallas_call(
        functools.partial(_attn_kernel, diff=diff, n_cache=n_cache, n_new=tok_per_batch, tq=tq,
                          lam_init=lam_init),
        grid=(n_batch, n_kv, nq),
        in_specs=in_specs,
        out_specs=pl.BlockSpec(q_block, lambda b, h, i: (h, b * nq + i, 0)),
        out_shape=jax.ShapeDtypeStruct((H_A, n, HEAD_W), BF16),
        scratch_shapes=scratch,
        compiler_params=_params(3),
        name=("attn_diff" if diff else "attn_gqa") + ("_cache" if n_cache else ""),
    )(*args)


def _layer_norm(y, g, b):
    mu = jnp.mean(y, axis=-1, keepdims=True)
    yc = y - mu
    var = jnp.mean(yc * yc, axis=-1, keepdims=True)
    return yc * lax.rsqrt(var + LN_EPS) * g + b


def _route(h2, wr_ref, rb_ref):
    logits = jnp.dot(h2, wr_ref[...], preferred_element_type=F32)
    scores = jax.nn.sigmoid(logits)
    biased = scores + rb_ref[...]
    lane = lax.broadcasted_iota(jnp.int32, biased.shape, 1)
    valid = lane < N_EXPERTS
    pos = lane & (EXPERTS_PER_GROUP - 1)

    def neighbour(kk):
        fwd = pltpu.roll(biased, LANES - kk, 1)
        bwd = pltpu.roll(biased, EXPERTS_PER_GROUP - kk, 1)
        wrapped = pos + kk >= EXPERTS_PER_GROUP
        return jnp.where(wrapped, bwd, fwd), wrapped

    (v1, w1), (v2, w2), (v3, w3) = neighbour(1), neighbour(2), neighbour(3)
    rank = jnp.zeros(biased.shape, jnp.int32)
    for v, wrapped in ((v1, w1), (v2, w2), (v3, w3)):
        ge = jnp.where(v >= biased, 1, 0)
        gt = jnp.where(v > biased, 1, 0)
        rank = rank + jnp.where(wrapped, ge, gt)
    hi_a, lo_a = jnp.maximum(biased, v1), jnp.minimum(biased, v1)
    hi_b, lo_b = jnp.maximum(v2, v3), jnp.minimum(v2, v3)
    top1 = jnp.maximum(hi_a, hi_b)
    top2 = jnp.maximum(jnp.minimum(hi_a, hi_b), jnp.maximum(lo_a, lo_b))
    grp_score = jnp.where(valid, top1 + top2, -jnp.inf)
    best = jnp.max(grp_score, axis=-1, keepdims=True)
    grp = (lane // EXPERTS_PER_GROUP).astype(F32)
    first = jnp.min(jnp.where(grp_score == best, grp, float(LANES)), axis=-1, keepdims=True)
    chosen = jnp.where(grp == first, rank, EXPERTS_PER_GROUP) < 2
    w_raw = jnp.where(chosen, scores, 0.0)
    return w_raw / jnp.sum(w_raw, axis=-1, keepdims=True)


def _post_kernel(x_ref, mod_ref, oa_ref, ob_ref, wga_ref, wgb_ref, wa_ref, wb_ref, wo_ref, lg_ref, lb_ref,
                 wr_ref, rb_ref, x1_ref, h2_ref, comb_ref):
    x = x_ref[...]
    sh1 = mod_ref[:, 0:D_MODEL]
    sc1 = mod_ref[:, D_MODEL:2 * D_MODEL]
    g1 = mod_ref[:, 2 * D_MODEL:3 * D_MODEL]
    sh2 = mod_ref[:, 3 * D_MODEL:4 * D_MODEL]
    sc2 = mod_ref[:, 4 * D_MODEL:5 * D_MODEL]
    h = (x * (1.0 + sc1) + sh1).astype(BF16)
    oa = jnp.concatenate([oa_ref[i] for i in range(H_A)], axis=1)
    ob = jnp.concatenate([ob_ref[i] for i in range(H_B)], axis=1)
    ga = jax.nn.sigmoid(jnp.dot(h, wga_ref[...], preferred_element_type=F32))
    merged = ga * jnp.dot(oa, wa_ref[...], preferred_element_type=F32)
    gb = jax.nn.sigmoid(jnp.dot(h, wgb_ref[...], preferred_element_type=F32))
    merged = merged + gb * jnp.dot(ob, wb_ref[...], preferred_element_type=F32)
    mix = jnp.dot(merged.astype(BF16), wo_ref[...], preferred_element_type=F32)
    x1 = _layer_norm(ALPHA * x + g1 * mix, lg_ref[...], lb_ref[...])
    x1_ref[...] = x1
    h2 = (x1 * (1.0 + sc2) + sh2).astype(BF16)
    h2_ref[...] = h2
    comb_ref[...] = _route(h2, wr_ref, rb_ref)


def _post(x, mod_rows, oa, ob, w_in_bf, wa_bf, wb_bf, wo_bf, ln_g, ln_b, wr_pad, rb_pad, layer, *,
          tok_per_batch, latent):
    n = x.shape[0]
    tm = TM_POST
    tiles_per_batch = tok_per_batch // tm
    const2 = lambda i: (0, 0)
    wspec = pl.BlockSpec((None, D_MODEL, D_MODEL), lambda i: (layer, 0, 0))
    return pl.pallas_call(
        _post_kernel,
        grid=(n // tm,),
        in_specs=[
            pl.BlockSpec((tm, D_MODEL), lambda i: (i, 0)),
            pl.BlockSpec((None, 1, 6 * D_MODEL), lambda i: (1 + i // tiles_per_batch if latent else 0, 0, 0)),
            pl.BlockSpec((H_A, tm, HEAD_W), lambda i: (0, i, 0)),
            pl.BlockSpec((H_B, tm, HEAD_W), lambda i: (0, i, 0)),
            pl.BlockSpec((None, D_MODEL, D_MODEL), lambda i: (layer, 0, W_QKV // D_MODEL)),
            pl.BlockSpec((None, D_MODEL, D_MODEL), lambda i: (layer, 0, W_QKV // D_MODEL + 1)),
            wspec, wspec, wspec,
            pl.BlockSpec((1, D_MODEL), const2),
            pl.BlockSpec((1, D_MODEL), const2),
            pl.BlockSpec((D_MODEL, LANES), const2),
            pl.BlockSpec((1, LANES), const2),
        ],
        out_specs=[
            pl.BlockSpec((tm, D_MODEL), lambda i: (i, 0)),
            pl.BlockSpec((tm, D_MODEL), lambda i: (i, 0)),
            pl.BlockSpec((tm, LANES), lambda i: (i, 0)),
        ],
        out_shape=[
            jax.ShapeDtypeStruct((n, D_MODEL), F32),
            jax.ShapeDtypeStruct((n, D_MODEL), BF16),
            jax.ShapeDtypeStruct((n, LANES), F32),
        ],
        compiler_params=_params(1),
        name="post_attn",
    )(x, mod_rows, oa, ob, w_in_bf, w_in_bf, wa_bf, wb_bf, wo_bf, ln_g, ln_b, wr_pad, rb_pad)


def _moe_kernel(x1_ref, h2_ref, comb_ref, mod_ref, wg_ref, wu_ref, wd_ref, lg_ref, lb_ref, o_ref, acc_ref):
    e = pl.program_id(1)

    @pl.when(e == 0)
    def _():
        acc_ref[...] = jnp.zeros_like(acc_ref)

    h2 = h2_ref[...]
    comb = comb_ref[...]
    lane = lax.broadcasted_iota(jnp.int32, comb.shape, 1)
    c = jnp.sum(jnp.where(lane == e, comb, 0.0), axis=-1, keepdims=True)
    g = jnp.dot(h2, wg_ref[...].astype(BF16), preferred_element_type=F32)
    u = jnp.dot(h2, wu_ref[...].astype(BF16), preferred_element_type=F32)
    act = (g * jax.nn.sigmoid(g)) * u * c
    acc_ref[...] += jnp.dot(act.astype(BF16), wd_ref[...].astype(BF16), preferred_element_type=F32)

    @pl.when(e == N_EXPERTS - 1)
    def _():
        g2 = mod_ref[:, 5 * D_MODEL:6 * D_MODEL]
        o_ref[...] = _layer_norm(ALPHA * x1_ref[...] + g2 * acc_ref[...], lg_ref[...], lb_ref[...])


def _moe(x1, h2, comb, mod_rows, w_gate, w_up, w_down, ln_g, ln_b, layer, *, tok_per_batch, latent):
    n = x1.shape[0]
    tm = min(TM_MOE, tok_per_batch)
    tiles_per_batch = tok_per_batch // tm
    const2 = lambda i, e: (0, 0)
    return pl.pallas_call(
        _moe_kernel,
        grid=(n // tm, N_EXPERTS),
        in_specs=[
            pl.BlockSpec((tm, D_MODEL), lambda i, e: (i, 0)),
            pl.BlockSpec((tm, D_MODEL), lambda i, e: (i, 0)),
            pl.BlockSpec((tm, LANES), lambda i, e: (i, 0)),
            pl.BlockSpec((None, 1, 6 * D_MODEL),
                         lambda i, e: (1 + i // tiles_per_batch if latent else 0, 0, 0)),
            pl.BlockSpec((None, None, D_MODEL, D_EXPERT), lambda i, e: (layer, e, 0, 0)),
            pl.BlockSpec((None, None, D_MODEL, D_EXPERT), lambda i, e: (layer, e, 0, 0)),
            pl.BlockSpec((None, None, D_EXPERT, D_MODEL), lambda i, e: (layer, e, 0, 0)),
            pl.BlockSpec((1, D_MODEL), const2),
            pl.BlockSpec((1, D_MODEL), const2),
        ],
        out_specs=pl.BlockSpec((tm, D_MODEL), lambda i, e: (i, 0)),
        out_shape=jax.ShapeDtypeStruct((n, D_MODEL), F32),
        scratch_shapes=[pltpu.VMEM((tm, D_MODEL), F32)],
        compiler_params=_params(2),
        name="moe_ln2",
    )(x1, h2, comb, mod_rows, w_gate, w_up, w_down, ln_g, ln_b)


def _lambda_init(layer):
    return 0.8 - 0.6 * math.exp(-0.3 * layer)


def kernel(x_prompt, x_sample, cache_a_k, cache_a_v, cache_b_k, cache_b_v, c, c_ctx, w_mod, b_mod, w_in,
           lam_q1, lam_k1, lam_q2, lam_k2, subln_g, qn_g, kn_g, w_br_a, w_br_b, w_out, ln1_g, ln1_b,
           ln2_g, ln2_b, w_router, router_bias, w_gate, w_up, w_down):
    batch, seq, _ = x_prompt.shape
    dec_batch, dec_seq, _ = x_sample.shape

    cond8 = jnp.zeros((8, D_MODEL), F32).at[0].set(c_ctx).at[1:1 + dec_batch].set(c)
    mod = _modulation(cond8, w_mod, b_mod).reshape(DEPTH, 8, 1, 6 * D_MODEL)

    w_in_bf = w_in.astype(BF16)
    wa_bf, wb_bf, wo_bf = w_br_a.astype(BF16), w_br_b.astype(BF16), w_out.astype(BF16)
    wr_pad = jnp.zeros((D_MODEL, LANES), BF16).at[:, :N_EXPERTS].set(w_router.astype(BF16))
    rb_pad = jnp.zeros((1, LANES), F32).at[0, :N_EXPERTS].set(router_bias)
    rope_tabs = _rope_tables(dec_seq, D_A) + _rope_tables(dec_seq, D_B)
    row = lambda a, l: a[l].reshape(1, -1)
    cak, cav, cbk, cbv = (a.reshape(*a.shape[:3], -1) for a in (cache_a_k, cache_a_v, cache_b_k, cache_b_v))

    def layer_step(x, l, state_bufs, latent):
        tok = dec_seq if latent else seq
        nb = dec_batch if latent else batch
        outs = _qkv(x, mod[l], w_in_bf, row(qn_g, l), row(kn_g, l), l, tok_per_batch=tok,
                    rope_tabs=rope_tabs if latent else None, state_bufs=state_bufs)
        qa, ka, va, qb, kb, vb = outs[:6]
        tq = 256
        oa = _attention(qa, ka, va, diff=True, n_batch=nb, tok_per_batch=tok, tq=tq, layer=l,
                        lam_init=_lambda_init(l), subln_g=row(subln_g, l),
                        lam_params=tuple(row(a, l) for a in (lam_q1, lam_k1, lam_q2, lam_k2)),
                        cache_k=cak if latent else None, cache_v=cav if latent else None)
        ob = _attention(qb, kb, vb, diff=False, n_batch=nb, tok_per_batch=tok, tq=tq, layer=l,
                        cache_k=cbk if latent else None, cache_v=cbv if latent else None)
        x1, h2, comb = _post(x, mod[l], oa, ob, w_in_bf, wa_bf, wb_bf, wo_bf, row(ln1_g, l), row(ln1_b, l),
                             wr_pad, rb_pad, l, tok_per_batch=tok, latent=latent)
        x2 = _moe(x1, h2, comb, mod[l], w_gate, w_up, w_down, row(ln2_g, l), row(ln2_b, l), l,
                  tok_per_batch=tok, latent=latent)
        return x2, (outs[6:] if state_bufs is not None else None)

    wa_state = H_A * HEAD_W
    wb_state = G_B * HEAD_W
    state = tuple(jnp.zeros((batch, DEPTH, seq, w), F32) for w in (wa_state, wa_state, wb_state, wb_state))
    y = x_prompt.reshape(batch * seq, D_MODEL)
    for l in range(DEPTH):
        y, state = layer_step(y, l, state, latent=False)
    y_prompt = y.reshape(batch, seq, D_MODEL)

    y = x_sample.reshape(dec_batch * dec_seq, D_MODEL)
    for l in range(DEPTH):
        y, _ = layer_step(y, l, None, latent=True)
    y_sample = y.reshape(dec_batch, dec_seq, D_MODEL)

    sak, sav, sbk, sbv = state
    return (y_prompt, y_sample,
            sak.reshape(batch, DEPTH, seq, H_A, 2 * D_A), sav.reshape(batch, DEPTH, seq, H_A, 2 * D_A),
            sbk.reshape(batch, DEPTH, seq, G_B, D_B), sbv.reshape(batch, DEPTH, seq, G_B, D_B))
```

```python
import functools
import math

import jax
import jax.numpy as jnp
from jax import lax
from jax.experimental import pallas as pl
from jax.experimental.pallas import tpu as pltpu

D_MODEL = 1024
DEPTH = 4
GRID_W = 64
ROPE_THETA = 10000.0
H_A = 8
D_A = 64
H_B = 8
G_B = 4
D_B = 128
HEAD_W = 128
N_EXPERTS = 16
EXPERTS_PER_GROUP = 4
D_EXPERT = 512
ALPHA = (2 * DEPTH) ** 0.25
LN_EPS = 1e-6
RMS_EPS = 1e-6
W_QKV = 3 * H_A * 2 * D_A + H_B * D_B + 2 * G_B * D_B
LOG2E = 1.4426950408889634
LANES = 128
VMEM_LIMIT = 56 * 1024 * 1024

BF16 = jnp.bfloat16
F32 = jnp.float32

TM_QKV = 256
TM_POST = 256
TM_MOE = 1024
TN_MOD = 1536
TQ_ATTN = 1024
KEY_CHUNK = 512


def _params(n_axes, vmem=VMEM_LIMIT):
    return pltpu.CompilerParams(dimension_semantics=("arbitrary",) * n_axes,
                                vmem_limit_bytes=vmem)


def _mod_kernel(c_ref, w_ref, b_ref, o_ref):
    c = c_ref[...]
    s = (c * jax.nn.sigmoid(c)).astype(BF16)
    o_ref[...] = jnp.dot(s, w_ref[...].astype(BF16), preferred_element_type=F32) + b_ref[...]


def _modulation(cond8, w_mod, b_mod):
    n_out = 6 * D_MODEL
    return pl.pallas_call(
        _mod_kernel,
        grid=(DEPTH, n_out // TN_MOD),
        in_specs=[
            pl.BlockSpec((8, D_MODEL), lambda l, j: (0, 0)),
            pl.BlockSpec((None, D_MODEL, TN_MOD), lambda l, j: (l, 0, j)),
            pl.BlockSpec((None, 1, TN_MOD), lambda l, j: (l, 0, j)),
        ],
        out_specs=pl.BlockSpec((None, 8, TN_MOD), lambda l, j: (l, 0, j)),
        out_shape=jax.ShapeDtypeStruct((DEPTH, 8, n_out), F32),
        compiler_params=_params(2),
        name="modulation",
    )(cond8, w_mod, b_mod.reshape(DEPTH, 1, n_out))


def _rope_tables(n_tok, d):
    nf = d // 4
    n_rows = n_tok // GRID_W
    row = jnp.repeat(jnp.arange(n_rows, dtype=F32), GRID_W)
    col = jnp.tile(jnp.arange(GRID_W, dtype=F32), n_rows)
    inv = ROPE_THETA ** (-jnp.arange(nf, dtype=F32) / nf)
    lane = jnp.arange(LANES) % d
    axis = lane // (2 * nf)
    half = (lane % (2 * nf)) // nf
    freq = lane % nf
    pos = jnp.where(axis[None, :] == 0, row[:, None], col[:, None])
    ang = pos * inv[freq][None, :]
    cos, sin = jnp.cos(ang), jnp.sin(ang)
    sin_lo = jnp.where(half[None, :] == 0, -sin, 0.0)
    sin_hi = jnp.where(half[None, :] == 1, sin, 0.0)
    return cos, sin_lo, sin_hi


def _rope(x, cos, sin_lo, sin_hi, nf):
    up = pltpu.roll(x, LANES - nf, 1)
    down = pltpu.roll(x, nf, 1)
    return x * cos + up * sin_lo + down * sin_hi


def _rms(x, g):
    ms = jnp.mean(x * x, axis=-1, keepdims=True)
    return x * lax.rsqrt(ms + RMS_EPS) * g


def _qkv_kernel(*refs, rope, states):
    x_ref, mod_ref, w_ref, qn_ref, kn_ref = refs[:5]
    pos = 5
    if rope:
        ca, sa_lo, sa_hi, cb, sb_lo, sb_hi = (r[...] for r in refs[pos:pos + 6])
        pos += 6
    if states:
        pos += 4
    qa_ref, ka_ref, va_ref, qb_ref, kb_ref, vb_ref = refs[pos:pos + 6]
    pos += 6
    if states:
        sak_ref, sav_ref, sbk_ref, sbv_ref = refs[pos:pos + 4]

    sh1 = mod_ref[:, 0:D_MODEL]
    sc1 = mod_ref[:, D_MODEL:2 * D_MODEL]
    h = (x_ref[...] * (1.0 + sc1) + sh1).astype(BF16)

    def seg(lo, width):
        return jnp.dot(h, w_ref[:, lo:lo + width], preferred_element_type=F32)

    def heads(p, n):
        return [p[:, i * HEAD_W:(i + 1) * HEAD_W] for i in range(n)]

    qa_scale = (D_A ** -0.5) * LOG2E
    qb_scale = (D_B ** -0.5) * LOG2E
    nfa, nfb = D_A // 4, D_B // 4

    p = seg(0, H_A * HEAD_W)
    for i, c in enumerate(heads(p, H_A)):
        if rope:
            c = _rope(c, ca, sa_lo, sa_hi, nfa)
        qa_ref[i] = (c * qa_scale).astype(BF16)

    p = seg(H_A * HEAD_W, H_A * HEAD_W)
    if states:
        sak_ref[...] = p
    for i, c in enumerate(heads(p, H_A)):
        if rope:
            c = _rope(c, ca, sa_lo, sa_hi, nfa)
        ka_ref[i] = c.astype(BF16)

    p = seg(2 * H_A * HEAD_W, H_A * HEAD_W)
    if states:
        sav_ref[...] = p
    for i, c in enumerate(heads(p, H_A)):
        va_ref[i] = c.astype(BF16)

    off = 3 * H_A * HEAD_W
    p = seg(off, H_B * HEAD_W)
    qn = qn_ref[...]
    for i, c in enumerate(heads(p, H_B)):
        c = _rms(c, qn)
        if rope:
            c = _rope(c, cb, sb_lo, sb_hi, nfb)
        qb_ref[i] = (c * qb_scale).astype(BF16)

    off += H_B * HEAD_W
    p = seg(off, G_B * HEAD_W)
    kn = kn_ref[...]
    for i, c in enumerate(heads(p, G_B)):
        c = _rms(c, kn)
        if states:
            sbk_ref[:, i * HEAD_W:(i + 1) * HEAD_W] = c
        if rope:
            c = _rope(c, cb, sb_lo, sb_hi, nfb)
        kb_ref[i] = c.astype(BF16)

    off += G_B * HEAD_W
    p = seg(off, G_B * HEAD_W)
    if states:
        sbv_ref[...] = p
    for i, c in enumerate(heads(p, G_B)):
        vb_ref[i] = c.astype(BF16)


def _qkv(x, mod_rows, w_in_bf, qn_g, kn_g, layer, *, tok_per_batch, rope_tabs=None, state_bufs=None):
    n = x.shape[0]
    tm = TM_QKV
    tiles_per_batch = tok_per_batch // tm
    rope = rope_tabs is not None
    states = state_bufs is not None
    row0 = 1 if rope else 0

    in_specs = [
        pl.BlockSpec((tm, D_MODEL), lambda i: (i, 0)),
        pl.BlockSpec((None, 1, 6 * D_MODEL), lambda i: (row0 + i // tiles_per_batch if rope else 0, 0, 0)),
        pl.BlockSpec((None, D_MODEL, W_QKV), lambda i: (layer, 0, 0)),
        pl.BlockSpec((1, HEAD_W), lambda i: (0, 0)),
        pl.BlockSpec((1, HEAD_W), lambda i: (0, 0)),
    ]
    args = [x, mod_rows, w_in_bf, qn_g, kn_g]
    if rope:
        in_specs += [pl.BlockSpec((tm, LANES), lambda i: (i % tiles_per_batch, 0))] * 6
        args += list(rope_tabs)
    aliases = {}
    if states:
        for k, buf in enumerate(state_bufs):
            aliases[len(args)] = 6 + k
            in_specs.append(pl.BlockSpec(memory_space=pl.ANY))
            args.append(buf)

    def hm(nh):
        return (pl.BlockSpec((nh, tm, HEAD_W), lambda i: (0, i, 0)),
                jax.ShapeDtypeStruct((nh, n, HEAD_W), BF16))

    outs = [hm(H_A), hm(H_A), hm(H_A), hm(H_B), hm(G_B), hm(G_B)]
    if states:
        assert tm == tok_per_batch
        for buf in state_bufs:
            outs.append((pl.BlockSpec((None, None, tm, buf.shape[-1]), lambda i: (i, layer, 0, 0)),
                         jax.ShapeDtypeStruct(buf.shape, F32)))
    return pl.pallas_call(
        functools.partial(_qkv_kernel, rope=rope, states=states),
        grid=(n // tm,),
        in_specs=in_specs,
        out_specs=[o[0] for o in outs],
        out_shape=[o[1] for o in outs],
        input_output_aliases=aliases,
        compiler_params=_params(1),
        name="qkv_states" if states else "qkv_rope",
    )(*args)


def _attn_kernel(*refs, diff, n_cache, n_new, tq, kc, lam_init):
    if diff:
        lq1, lk1, lq2, lk2, g_ref = refs[:5]
        refs = refs[5:]
    q_ref, kn_ref, vn_ref = refs[:3]
    refs = refs[3:]
    if n_cache:
        kc_ref, vc_ref = refs[:2]
        refs = refs[2:]
    o_ref = refs[0]
    n_keys = n_cache + n_new

    def transposed(x):
        return x.astype(F32).T.astype(BF16)

    if n_cache:
        k_all, vt_all = refs[1:3]

        @pl.when(pl.program_id(2) == 0)
        def _():
            k_all[0:n_cache, :] = kc_ref[...].astype(BF16)
            k_all[n_cache:n_keys, :] = kn_ref[...]
            vt_all[:, 0:n_cache] = transposed(vc_ref[...])
            vt_all[:, n_cache:n_keys] = transposed(vn_ref[...])

        k_chunk = lambda c: k_all[c * kc:(c + 1) * kc, :]
        vt_chunk = lambda c: vt_all[:, c * kc:(c + 1) * kc]
    else:
        vt = transposed(vn_ref[...])
        k_chunk = lambda c: kn_ref[c * kc:(c + 1) * kc, :]
        vt_chunk = lambda c: vt[:, c * kc:(c + 1) * kc]

    if diff:
        qt = q_ref[...].astype(F32).T
        row = lax.broadcasted_iota(jnp.int32, qt.shape, 0)
        qt = jnp.concatenate([jnp.where(row < D_A, qt, 0.0), jnp.where(row >= D_A, qt, 0.0)], axis=1)
    else:
        qt = jnp.concatenate([q_ref[0].astype(F32).T, q_ref[1].astype(F32).T], axis=1)
    qt = qt.astype(BF16)

    def scores(c):
        return jnp.dot(k_chunk(c), qt, preferred_element_type=F32)

    n_chunks = n_keys // kc
    m = l = acc = None
    s_next = scores(0)
    for c in range(n_chunks):
        s = s_next
        if c + 1 < n_chunks:
            s_next = scores(c + 1)
        m_c = jnp.max(s, axis=0, keepdims=True)
        m_new = m_c if c == 0 else jnp.maximum(m, m_c)
        e = jnp.exp2(s - m_new)
        l_c = jnp.sum(e, axis=0, keepdims=True)
        pv = jnp.dot(vt_chunk(c), e.astype(BF16), preferred_element_type=F32)
        if c == 0:
            l, acc = l_c, pv
        else:
            alpha = jnp.exp2(m - m_new)
            l = alpha * l + l_c
            acc = alpha * acc + pv
        m = m_new

    inv = 1.0 / l
    if diff:
        lam = (jnp.exp(jnp.sum(lq1[...] * lk1[...], axis=-1, keepdims=True))
               - jnp.exp(jnp.sum(lq2[...] * lk2[...], axis=-1, keepdims=True)) + lam_init)
        o = acc[:, :tq] * inv[:, :tq] - acc[:, tq:] * (lam * inv[:, tq:])
        o = _rms(o.T, g_ref[...]) * (1.0 - lam_init)
        o_ref[...] = o.astype(BF16)
    else:
        o = acc * inv
        o_ref[0] = o[:, :tq].T.astype(BF16)
        o_ref[1] = o[:, tq:].T.astype(BF16)


def _attention(q, k_new, v_new, *, diff, n_batch, tok_per_batch, tq, layer, lam_init=0.0,
               lam_params=None, subln_g=None, cache_k=None, cache_v=None):
    n_kv = k_new.shape[0]
    n = q.shape[1]
    nq = tok_per_batch // tq
    n_cache = 0 if cache_k is None else cache_k.shape[2]
    n_keys = n_cache + tok_per_batch
    kc = math.gcd(n_keys, KEY_CHUNK)

    in_specs, args = [], []
    if diff:
        in_specs += [pl.BlockSpec((1, D_A), lambda b, h, i: (0, 0))] * 4
        args += list(lam_params)
        in_specs.append(pl.BlockSpec((1, HEAD_W), lambda b, h, i: (0, 0)))
        args.append(subln_g)
    q_block = (None, tq, HEAD_W) if diff else (2, tq, HEAD_W)
    in_specs += [
        pl.BlockSpec(q_block, lambda b, h, i: (h, b * nq + i, 0)),
        pl.BlockSpec((None, tok_per_batch, HEAD_W), lambda b, h, i: (h, b, 0)),
        pl.BlockSpec((None, tok_per_batch, HEAD_W), lambda b, h, i: (h, b, 0)),
    ]
    args += [q, k_new, v_new]
    scratch = []
    if n_cache:
        cspec = pl.BlockSpec((None, None, n_cache, HEAD_W), lambda b, h, i: (b, layer, 0, h))
        in_specs += [cspec, cspec]
        args += [cache_k, cache_v]
        scratch = [pltpu.VMEM((n_keys, HEAD_W), BF16), pltpu.VMEM((HEAD_W, n_keys), BF16)]
    return pl.pallas_call(
        functools.partial(_attn_kernel, diff=diff, n_cache=n_cache, n_new=tok_per_batch, tq=tq, kc=kc,
                          lam_init=lam_init),
        grid=(n_batch, n_kv, nq),
        in_specs=in_specs,
        out_specs=pl.BlockSpec(q_block, lambda b, h, i: (h, b * nq + i, 0)),
        out_shape=jax.ShapeDtypeStruct((H_A, n, HEAD_W), BF16),
        scratch_shapes=scratch,
        compiler_params=_params(3),
        name=("attn_diff" if diff else "attn_gqa") + ("_cache" if n_cache else ""),
    )(*args)


def _layer_norm(y, g, b):
    mu = jnp.mean(y, axis=-1, keepdims=True)
    yc = y - mu
    var = jnp.mean(yc * yc, axis=-1, keepdims=True)
    return yc * lax.rsqrt(var + LN_EPS) * g + b


def _route(h2, wr_ref, rb_ref):
    logits = jnp.dot(h2, wr_ref[...], preferred_element_type=F32)
    scores = jax.nn.sigmoid(logits)
    biased = scores + rb_ref[...]
    lane = lax.broadcasted_iota(jnp.int32, biased.shape, 1)
    valid = lane < N_EXPERTS
    pos = lane & (EXPERTS_PER_GROUP - 1)

    def neighbour(kk):
        fwd = pltpu.roll(biased, LANES - kk, 1)
        bwd = pltpu.roll(biased, EXPERTS_PER_GROUP - kk, 1)
        wrapped = pos + kk >= EXPERTS_PER_GROUP
        return jnp.where(wrapped, bwd, fwd), wrapped

    (v1, w1), (v2, w2), (v3, w3) = neighbour(1), neighbour(2), neighbour(3)
    rank = jnp.zeros(biased.shape, jnp.int32)
    for v, wrapped in ((v1, w1), (v2, w2), (v3, w3)):
        ge = jnp.where(v >= biased, 1, 0)
        gt = jnp.where(v > biased, 1, 0)
        rank = rank + jnp.where(wrapped, ge, gt)
    hi_a, lo_a = jnp.maximum(biased, v1), jnp.minimum(biased, v1)
    hi_b, lo_b = jnp.maximum(v2, v3), jnp.minimum(v2, v3)
    top1 = jnp.maximum(hi_a, hi_b)
    top2 = jnp.maximum(jnp.minimum(hi_a, hi_b), jnp.maximum(lo_a, lo_b))
    grp_score = jnp.where(valid, top1 + top2, -jnp.inf)
    best = jnp.max(grp_score, axis=-1, keepdims=True)
    grp = (lane // EXPERTS_PER_GROUP).astype(F32)
    first = jnp.min(jnp.where(grp_score == best, grp, float(LANES)), axis=-1, keepdims=True)
    chosen = jnp.where(grp == first, rank, EXPERTS_PER_GROUP) < 2
    w_raw = jnp.where(chosen, scores, 0.0)
    return w_raw / jnp.sum(w_raw, axis=-1, keepdims=True)


def _post_kernel(x_ref, mod_ref, oa_ref, ob_ref, wga_ref, wgb_ref, wa_ref, wb_ref, wo_ref, lg_ref, lb_ref,
                 wr_ref, rb_ref, x1_ref, h2_ref, comb_ref):
    x = x_ref[...]
    sh1 = mod_ref[:, 0:D_MODEL]
    sc1 = mod_ref[:, D_MODEL:2 * D_MODEL]
    g1 = mod_ref[:, 2 * D_MODEL:3 * D_MODEL]
    sh2 = mod_ref[:, 3 * D_MODEL:4 * D_MODEL]
    sc2 = mod_ref[:, 4 * D_MODEL:5 * D_MODEL]
    h = (x * (1.0 + sc1) + sh1).astype(BF16)
    oa = jnp.concatenate([oa_ref[i] for i in range(H_A)], axis=1)
    ob = jnp.concatenate([ob_ref[i] for i in range(H_B)], axis=1)
    ga = jax.nn.sigmoid(jnp.dot(h, wga_ref[...], preferred_element_type=F32))
    merged = ga * jnp.dot(oa, wa_ref[...], preferred_element_type=F32)
    gb = jax.nn.sigmoid(jnp.dot(h, wgb_ref[...], preferred_element_type=F32))
    merged = merged + gb * jnp.dot(ob, wb_ref[...], preferred_element_type=F32)
    mix = jnp.dot(merged.astype(BF16), wo_ref[...], preferred_element_type=F32)
    x1 = _layer_norm(ALPHA * x + g1 * mix, lg_ref[...], lb_ref[...])
    x1_ref[...] = x1
    h2 = (x1 * (1.0 + sc2) + sh2).astype(BF16)
    h2_ref[...] = h2
    comb_ref[...] = _route(h2, wr_ref, rb_ref)


def _post(x, mod_rows, oa, ob, w_in_bf, wa_bf, wb_bf, wo_bf, ln_g, ln_b, wr_pad, rb_pad, layer, *,
          tok_per_batch, latent):
    n = x.shape[0]
    tm = TM_POST
    tiles_per_batch = tok_per_batch // tm
    const2 = lambda i: (0, 0)
    wspec = pl.BlockSpec((None, D_MODEL, D_MODEL), lambda i: (layer, 0, 0))
    return pl.pallas_call(
        _post_kernel,
        grid=(n // tm,),
        in_specs=[
            pl.BlockSpec((tm, D_MODEL), lambda i: (i, 0)),
            pl.BlockSpec((None, 1, 6 * D_MODEL), lambda i: (1 + i // tiles_per_batch if latent else 0, 0, 0)),
            pl.BlockSpec((H_A, tm, HEAD_W), lambda i: (0, i, 0)),
            pl.BlockSpec((H_B, tm, HEAD_W), lambda i: (0, i, 0)),
            pl.BlockSpec((None, D_MODEL, D_MODEL), lambda i: (layer, 0, W_QKV // D_MODEL)),
            pl.BlockSpec((None, D_MODEL, D_MODEL), lambda i: (layer, 0, W_QKV // D_MODEL + 1)),
            wspec, wspec, wspec,
            pl.BlockSpec((1, D_MODEL), const2),
            pl.BlockSpec((1, D_MODEL), const2),
            pl.BlockSpec((D_MODEL, LANES), const2),
            pl.BlockSpec((1, LANES), const2),
        ],
        out_specs=[
            pl.BlockSpec((tm, D_MODEL), lambda i: (i, 0)),
            pl.BlockSpec((tm, D_MODEL), lambda i: (i, 0)),
            pl.BlockSpec((tm, LANES), lambda i: (i, 0)),
        ],
        out_shape=[
            jax.ShapeDtypeStruct((n, D_MODEL), F32),
            jax.ShapeDtypeStruct((n, D_MODEL), BF16),
            jax.ShapeDtypeStruct((n, LANES), F32),
        ],
        compiler_params=_params(1),
        name="post_attn",
    )(x, mod_rows, oa, ob, w_in_bf, w_in_bf, wa_bf, wb_bf, wo_bf, ln_g, ln_b, wr_pad, rb_pad)


def _moe_kernel(x1_ref, h2_ref, comb_ref, mod_ref, wg_ref, wu_ref, wd_ref, lg_ref, lb_ref, o_ref, acc_ref):
    e = pl.program_id(1)

    @pl.when(e == 0)
    def _():
        acc_ref[...] = jnp.zeros_like(acc_ref)

    h2 = h2_ref[...]
    comb = comb_ref[...]
    lane = lax.broadcasted_iota(jnp.int32, comb.shape, 1)
    c = jnp.sum(jnp.where(lane == e, comb, 0.0), axis=-1, keepdims=True)
    g = jnp.dot(h2, wg_ref[...].astype(BF16), preferred_element_type=F32)
    u = jnp.dot(h2, wu_ref[...].astype(BF16), preferred_element_type=F32)
    act = (g * jax.nn.sigmoid(g)) * u * c
    acc_ref[...] += jnp.dot(act.astype(BF16), wd_ref[...].astype(BF16), preferred_element_type=F32)

    @pl.when(e == N_EXPERTS - 1)
    def _():
        g2 = mod_ref[:, 5 * D_MODEL:6 * D_MODEL]
        o_ref[...] = _layer_norm(ALPHA * x1_ref[...] + g2 * acc_ref[...], lg_ref[...], lb_ref[...])


def _moe(x1, h2, comb, mod_rows, w_gate, w_up, w_down, ln_g, ln_b, layer, *, tok_per_batch, latent):
    n = x1.shape[0]
    tm = min(TM_MOE, tok_per_batch if latent else n)
    tiles_per_batch = max(tok_per_batch // tm, 1)
    const2 = lambda i, e: (0, 0)
    return pl.pallas_call(
        _moe_kernel,
        grid=(n // tm, N_EXPERTS),
        in_specs=[
            pl.BlockSpec((tm, D_MODEL), lambda i, e: (i, 0)),
            pl.BlockSpec((tm, D_MODEL), lambda i, e: (i, 0)),
            pl.BlockSpec((tm, LANES), lambda i, e: (i, 0)),
            pl.BlockSpec((None, 1, 6 * D_MODEL),
                         lambda i, e: (1 + i // tiles_per_batch if latent else 0, 0, 0)),
            pl.BlockSpec((None, None, D_MODEL, D_EXPERT), lambda i, e: (layer, e, 0, 0)),
            pl.BlockSpec((None, None, D_MODEL, D_EXPERT), lambda i, e: (layer, e, 0, 0)),
            pl.BlockSpec((None, None, D_EXPERT, D_MODEL), lambda i, e: (layer, e, 0, 0)),
            pl.BlockSpec((1, D_MODEL), const2),
            pl.BlockSpec((1, D_MODEL), const2),
        ],
        out_specs=pl.BlockSpec((tm, D_MODEL), lambda i, e: (i, 0)),
        out_shape=jax.ShapeDtypeStruct((n, D_MODEL), F32),
        scratch_shapes=[pltpu.VMEM((tm, D_MODEL), F32)],
        compiler_params=_params(2),
        name="moe_ln2",
    )(x1, h2, comb, mod_rows, w_gate, w_up, w_down, ln_g, ln_b)


def _lambda_init(layer):
    return 0.8 - 0.6 * math.exp(-0.3 * layer)


def kernel(x_prompt, x_sample, cache_a_k, cache_a_v, cache_b_k, cache_b_v, c, c_ctx, w_mod, b_mod, w_in,
           lam_q1, lam_k1, lam_q2, lam_k2, subln_g, qn_g, kn_g, w_br_a, w_br_b, w_out, ln1_g, ln1_b,
           ln2_g, ln2_b, w_router, router_bias, w_gate, w_up, w_down):
    batch, seq, _ = x_prompt.shape
    dec_batch, dec_seq, _ = x_sample.shape

    cond8 = jnp.zeros((8, D_MODEL), F32).at[0].set(c_ctx).at[1:1 + dec_batch].set(c)
    mod = _modulation(cond8, w_mod, b_mod).reshape(DEPTH, 8, 1, 6 * D_MODEL)

    w_in_bf = w_in.astype(BF16)
    wa_bf, wb_bf, wo_bf = w_br_a.astype(BF16), w_br_b.astype(BF16), w_out.astype(BF16)
    wr_pad = jnp.zeros((D_MODEL, LANES), BF16).at[:, :N_EXPERTS].set(w_router.astype(BF16))
    rb_pad = jnp.zeros((1, LANES), F32).at[0, :N_EXPERTS].set(router_bias)
    rope_tabs = _rope_tables(dec_seq, D_A) + _rope_tables(dec_seq, D_B)
    row = lambda a, l: a[l].reshape(1, -1)
    cak, cav, cbk, cbv = (a.reshape(*a.shape[:3], -1) for a in (cache_a_k, cache_a_v, cache_b_k, cache_b_v))

    def layer_step(x, l, state_bufs, latent):
        tok = dec_seq if latent else seq
        nb = dec_batch if latent else batch
        outs = _qkv(x, mod[l], w_in_bf, row(qn_g, l), row(kn_g, l), l, tok_per_batch=tok,
                    rope_tabs=rope_tabs if latent else None, state_bufs=state_bufs)
        qa, ka, va, qb, kb, vb = outs[:6]
        tq = min(TQ_ATTN, tok)
        oa = _attention(qa, ka, va, diff=True, n_batch=nb, tok_per_batch=tok, tq=tq, layer=l,
                        lam_init=_lambda_init(l), subln_g=row(subln_g, l),
                        lam_params=tuple(row(a, l) for a in (lam_q1, lam_k1, lam_q2, lam_k2)),
                        cache_k=cak if latent else None, cache_v=cav if latent else None)
        ob = _attention(qb, kb, vb, diff=False, n_batch=nb, tok_per_batch=tok, tq=tq, layer=l,
                        cache_k=cbk if latent else None, cache_v=cbv if latent else None)
        x1, h2, comb = _post(x, mod[l], oa, ob, w_in_bf, wa_bf, wb_bf, wo_bf, row(ln1_g, l), row(ln1_b, l),
                             wr_pad, rb_pad, l, tok_per_batch=tok, latent=latent)
        x2 = _moe(x1, h2, comb, mod[l], w_gate, w_up, w_down, row(ln2_g, l), row(ln2_b, l), l,
                  tok_per_batch=tok, latent=latent)
        return x2, (outs[6:] if state_bufs is not None else None)

    wa_state = H_A * HEAD_W
    wb_state = G_B * HEAD_W
    state = tuple(jnp.zeros((batch, DEPTH, seq, w), F32) for w in (wa_state, wa_state, wb_state, wb_state))
    y = x_prompt.reshape(batch * seq, D_MODEL)
    for l in range(DEPTH):
        y, state = layer_step(y, l, state, latent=False)
    y_prompt = y.reshape(batch, seq, D_MODEL)

    y = x_sample.reshape(dec_batch * dec_seq, D_MODEL)
    for l in range(DEPTH):
        y, _ = layer_step(y, l, None, latent=True)
    y_sample = y.reshape(dec_batch, dec_seq, D_MODEL)

    sak, sav, sbk, sbv = state
    return (y_prompt, y_sample,
            sak.reshape(batch, DEPTH, seq, H_A, 2 * D_A), sav.reshape(batch, DEPTH, seq, H_A, 2 * D_A),
            sbk.reshape(batch, DEPTH, seq, G_B, D_B), sbv.reshape(batch, DEPTH, seq, G_B, D_B))
```

```python
import functools
import math

import jax
import jax.numpy as jnp
from jax import lax
from jax.experimental import pallas as pl
from jax.experimental.pallas import tpu as pltpu

D_MODEL = 1024
DEPTH = 4
GRID_W = 64
ROPE_THETA = 10000.0
H_A = 8
D_A = 64
H_B = 8
G_B = 4
D_B = 128
HEAD_W = 128
N_EXPERTS = 16
EXPERTS_PER_GROUP = 4
D_EXPERT = 512
ALPHA = (2 * DEPTH) ** 0.25
LN_EPS = 1e-6
RMS_EPS = 1e-6
W_QKV = 3 * H_A * 2 * D_A + H_B * D_B + 2 * G_B * D_B
LOG2E = 1.4426950408889634
LANES = 128
VMEM_LIMIT = 56 * 1024 * 1024

BF16 = jnp.bfloat16
F32 = jnp.float32

TM_QKV = 256
TM_POST = 256
TM_MOE = 1024
TN_MOD = 1536
TOP_K = 2
SORT_ALIGN = 16
MOE_CHUNK = 160
PERM_CHUNK = 256
UNPERM_CHUNK = 768
TQ_ATTN = 1024
KEY_CHUNK = 512


def _params(n_axes, vmem=VMEM_LIMIT):
    return pltpu.CompilerParams(dimension_semantics=("arbitrary",) * n_axes,
                                vmem_limit_bytes=vmem)


def _mod_kernel(c_ref, w_ref, b_ref, o_ref):
    c = c_ref[...]
    s = (c * jax.nn.sigmoid(c)).astype(BF16)
    o_ref[...] = jnp.dot(s, w_ref[...].astype(BF16), preferred_element_type=F32) + b_ref[...]


def _modulation(cond8, w_mod, b_mod):
    n_out = 6 * D_MODEL
    return pl.pallas_call(
        _mod_kernel,
        grid=(DEPTH, n_out // TN_MOD),
        in_specs=[
            pl.BlockSpec((8, D_MODEL), lambda l, j: (0, 0)),
            pl.BlockSpec((None, D_MODEL, TN_MOD), lambda l, j: (l, 0, j)),
            pl.BlockSpec((None, 1, TN_MOD), lambda l, j: (l, 0, j)),
        ],
        out_specs=pl.BlockSpec((None, 8, TN_MOD), lambda l, j: (l, 0, j)),
        out_shape=jax.ShapeDtypeStruct((DEPTH, 8, n_out), F32),
        compiler_params=_params(2),
        name="modulation",
    )(cond8, w_mod, b_mod.reshape(DEPTH, 1, n_out))


def _rope_tables(n_tok, d):
    nf = d // 4
    n_rows = n_tok // GRID_W
    row = jnp.repeat(jnp.arange(n_rows, dtype=F32), GRID_W)
    col = jnp.tile(jnp.arange(GRID_W, dtype=F32), n_rows)
    inv = ROPE_THETA ** (-jnp.arange(nf, dtype=F32) / nf)
    lane = jnp.arange(LANES) % d
    axis = lane // (2 * nf)
    half = (lane % (2 * nf)) // nf
    freq = lane % nf
    pos = jnp.where(axis[None, :] == 0, row[:, None], col[:, None])
    ang = pos * inv[freq][None, :]
    cos, sin = jnp.cos(ang), jnp.sin(ang)
    sin_lo = jnp.where(half[None, :] == 0, -sin, 0.0)
    sin_hi = jnp.where(half[None, :] == 1, sin, 0.0)
    return cos, sin_lo, sin_hi


def _rope(x, cos, sin_lo, sin_hi, nf):
    up = pltpu.roll(x, LANES - nf, 1)
    down = pltpu.roll(x, nf, 1)
    return x * cos + up * sin_lo + down * sin_hi


def _rms(x, g):
    ms = jnp.mean(x * x, axis=-1, keepdims=True)
    return x * lax.rsqrt(ms + RMS_EPS) * g


def _qkv_kernel(*refs, rope, states):
    x_ref, mod_ref, w_ref, qn_ref, kn_ref = refs[:5]
    pos = 5
    if rope:
        ca, sa_lo, sa_hi, cb, sb_lo, sb_hi = (r[...] for r in refs[pos:pos + 6])
        pos += 6
    if states:
        pos += 4
    qa_ref, ka_ref, va_ref, qb_ref, kb_ref, vb_ref = refs[pos:pos + 6]
    pos += 6
    if states:
        sak_ref, sav_ref, sbk_ref, sbv_ref = refs[pos:pos + 4]

    sh1 = mod_ref[:, 0:D_MODEL]
    sc1 = mod_ref[:, D_MODEL:2 * D_MODEL]
    h = (x_ref[...] * (1.0 + sc1) + sh1).astype(BF16)

    def seg(lo, width):
        return jnp.dot(h, w_ref[:, lo:lo + width], preferred_element_type=F32)

    def heads(p, n):
        return [p[:, i * HEAD_W:(i + 1) * HEAD_W] for i in range(n)]

    qa_scale = (D_A ** -0.5) * LOG2E
    qb_scale = (D_B ** -0.5) * LOG2E
    nfa, nfb = D_A // 4, D_B // 4

    p = seg(0, H_A * HEAD_W)
    for i, c in enumerate(heads(p, H_A)):
        if rope:
            c = _rope(c, ca, sa_lo, sa_hi, nfa)
        qa_ref[i] = (c * qa_scale).astype(BF16)

    p = seg(H_A * HEAD_W, H_A * HEAD_W)
    if states:
        sak_ref[...] = p
    for i, c in enumerate(heads(p, H_A)):
        if rope:
            c = _rope(c, ca, sa_lo, sa_hi, nfa)
        ka_ref[i] = c.astype(BF16)

    p = seg(2 * H_A * HEAD_W, H_A * HEAD_W)
    if states:
        sav_ref[...] = p
    for i, c in enumerate(heads(p, H_A)):
        va_ref[i] = c.astype(BF16)

    off = 3 * H_A * HEAD_W
    p = seg(off, H_B * HEAD_W)
    qn = qn_ref[...]
    for i, c in enumerate(heads(p, H_B)):
        c = _rms(c, qn)
        if rope:
            c = _rope(c, cb, sb_lo, sb_hi, nfb)
        qb_ref[i] = (c * qb_scale).astype(BF16)

    off += H_B * HEAD_W
    p = seg(off, G_B * HEAD_W)
    kn = kn_ref[...]
    for i, c in enumerate(heads(p, G_B)):
        c = _rms(c, kn)
        if states:
            sbk_ref[:, i * HEAD_W:(i + 1) * HEAD_W] = c
        if rope:
            c = _rope(c, cb, sb_lo, sb_hi, nfb)
        kb_ref[i] = c.astype(BF16)

    off += G_B * HEAD_W
    p = seg(off, G_B * HEAD_W)
    if states:
        sbv_ref[...] = p
    for i, c in enumerate(heads(p, G_B)):
        vb_ref[i] = c.astype(BF16)


def _qkv(x, mod_rows, w_in_bf, qn_g, kn_g, layer, *, tok_per_batch, rope_tabs=None, state_bufs=None):
    n = x.shape[0]
    tm = TM_QKV
    tiles_per_batch = tok_per_batch // tm
    rope = rope_tabs is not None
    states = state_bufs is not None
    row0 = 1 if rope else 0

    in_specs = [
        pl.BlockSpec((tm, D_MODEL), lambda i: (i, 0)),
        pl.BlockSpec((None, 1, 6 * D_MODEL), lambda i: (row0 + i // tiles_per_batch if rope else 0, 0, 0)),
        pl.BlockSpec((None, D_MODEL, W_QKV), lambda i: (layer, 0, 0)),
        pl.BlockSpec((1, HEAD_W), lambda i: (0, 0)),
        pl.BlockSpec((1, HEAD_W), lambda i: (0, 0)),
    ]
    args = [x, mod_rows, w_in_bf, qn_g, kn_g]
    if rope:
        in_specs += [pl.BlockSpec((tm, LANES), lambda i: (i % tiles_per_batch, 0))] * 6
        args += list(rope_tabs)
    aliases = {}
    if states:
        for k, buf in enumerate(state_bufs):
            aliases[len(args)] = 6 + k
            in_specs.append(pl.BlockSpec(memory_space=pl.ANY))
            args.append(buf)

    def hm(nh):
        return (pl.BlockSpec((nh, tm, HEAD_W), lambda i: (0, i, 0)),
                jax.ShapeDtypeStruct((nh, n, HEAD_W), BF16))

    outs = [hm(H_A), hm(H_A), hm(H_A), hm(H_B), hm(G_B), hm(G_B)]
    if states:
        assert tm == tok_per_batch
        for buf in state_bufs:
            outs.append((pl.BlockSpec((None, None, tm, buf.shape[-1]), lambda i: (i, layer, 0, 0)),
                         jax.ShapeDtypeStruct(buf.shape, F32)))
    return pl.pallas_call(
        functools.partial(_qkv_kernel, rope=rope, states=states),
        grid=(n // tm,),
        in_specs=in_specs,
        out_specs=[o[0] for o in outs],
        out_shape=[o[1] for o in outs],
        input_output_aliases=aliases,
        compiler_params=_params(1),
        name="qkv_states" if states else "qkv_rope",
    )(*args)


def _attn_kernel(*refs, diff, n_cache, n_new, tq, kc, lam_init):
    if diff:
        lq1, lk1, lq2, lk2, g_ref = refs[:5]
        refs = refs[5:]
    q_ref, kn_ref, vn_ref = refs[:3]
    refs = refs[3:]
    if n_cache:
        kc_ref, vc_ref = refs[:2]
        refs = refs[2:]
    o_ref = refs[0]
    n_keys = n_cache + n_new

    def transposed(x):
        return x.astype(F32).T.astype(BF16)

    if n_cache:
        k_all, vt_all = refs[1:3]

        @pl.when(pl.program_id(2) == 0)
        def _():
            k_all[0:n_cache, :] = kc_ref[...].astype(BF16)
            k_all[n_cache:n_keys, :] = kn_ref[...]
            vt_all[:, 0:n_cache] = transposed(vc_ref[...])
            vt_all[:, n_cache:n_keys] = transposed(vn_ref[...])

        k_chunk = lambda c: k_all[c * kc:(c + 1) * kc, :]
        vt_chunk = lambda c: vt_all[:, c * kc:(c + 1) * kc]
    else:
        vt = transposed(vn_ref[...])
        k_chunk = lambda c: kn_ref[c * kc:(c + 1) * kc, :]
        vt_chunk = lambda c: vt[:, c * kc:(c + 1) * kc]

    if diff:
        qt = q_ref[...].astype(F32).T
        row = lax.broadcasted_iota(jnp.int32, qt.shape, 0)
        qt = jnp.concatenate([jnp.where(row < D_A, qt, 0.0), jnp.where(row >= D_A, qt, 0.0)], axis=1)
    else:
        qt = jnp.concatenate([q_ref[0].astype(F32).T, q_ref[1].astype(F32).T], axis=1)
    qt = qt.astype(BF16)

    def scores(c):
        return jnp.dot(k_chunk(c), qt, preferred_element_type=F32)

    n_chunks = n_keys // kc
    m = l = acc = None
    s_next = scores(0)
    for c in range(n_chunks):
        s = s_next
        if c + 1 < n_chunks:
            s_next = scores(c + 1)
        m_c = jnp.max(s, axis=0, keepdims=True)
        m_new = m_c if c == 0 else jnp.maximum(m, m_c)
        e = jnp.exp2(s - m_new)
        l_c = jnp.sum(e, axis=0, keepdims=True)
        pv = jnp.dot(vt_chunk(c), e.astype(BF16), preferred_element_type=F32)
        if c == 0:
            l, acc = l_c, pv
        else:
            alpha = jnp.exp2(m - m_new)
            l = alpha * l + l_c
            acc = alpha * acc + pv
        m = m_new

    inv = 1.0 / l
    if diff:
        lam = (jnp.exp(jnp.sum(lq1[...] * lk1[...], axis=-1, keepdims=True))
               - jnp.exp(jnp.sum(lq2[...] * lk2[...], axis=-1, keepdims=True)) + lam_init)
        o = acc[:, :tq] * inv[:, :tq] - acc[:, tq:] * (lam * inv[:, tq:])
        o = _rms(o.T, g_ref[...]) * (1.0 - lam_init)
        o_ref[...] = o.astype(BF16)
    else:
        o = acc * inv
        o_ref[0] = o[:, :tq].T.astype(BF16)
        o_ref[1] = o[:, tq:].T.astype(BF16)


def _attention(q, k_new, v_new, *, diff, n_batch, tok_per_batch, tq, layer, lam_init=0.0,
               lam_params=None, subln_g=None, cache_k=None, cache_v=None):
    n_kv = k_new.shape[0]
    n = q.shape[1]
    nq = tok_per_batch // tq
    n_cache = 0 if cache_k is None else cache_k.shape[2]
    n_keys = n_cache + tok_per_batch
    kc = math.gcd(n_keys, KEY_CHUNK)

    in_specs, args = [], []
    if diff:
        in_specs += [pl.BlockSpec((1, D_A), lambda b, h, i: (0, 0))] * 4
        args += list(lam_params)
        in_specs.append(pl.BlockSpec((1, HEAD_W), lambda b, h, i: (0, 0)))
        args.append(subln_g)
    q_block = (None, tq, HEAD_W) if diff else (2, tq, HEAD_W)
    in_specs += [
        pl.BlockSpec(q_block, lambda b, h, i: (h, b * nq + i, 0)),
        pl.BlockSpec((None, tok_per_batch, HEAD_W), lambda b, h, i: (h, b, 0)),
        pl.BlockSpec((None, tok_per_batch, HEAD_W), lambda b, h, i: (h, b, 0)),
    ]
    args += [q, k_new, v_new]
    scratch = []
    if n_cache:
        cspec = pl.BlockSpec((None, None, n_cache, HEAD_W), lambda b, h, i: (b, layer, 0, h))
        in_specs += [cspec, cspec]
        args += [cache_k, cache_v]
        scratch = [pltpu.VMEM((n_keys, HEAD_W), BF16), pltpu.VMEM((HEAD_W, n_keys), BF16)]
    return pl.pallas_call(
        functools.partial(_attn_kernel, diff=diff, n_cache=n_cache, n_new=tok_per_batch, tq=tq, kc=kc,
                          lam_init=lam_init),
        grid=(n_batch, n_kv, nq),
        in_specs=in_specs,
        out_specs=pl.BlockSpec(q_block, lambda b, h, i: (h, b * nq + i, 0)),
        out_shape=jax.ShapeDtypeStruct((H_A, n, HEAD_W), BF16),
        scratch_shapes=scratch,
        compiler_params=_params(3),
        name=("attn_diff" if diff else "attn_gqa") + ("_cache" if n_cache else ""),
    )(*args)


def _layer_norm(y, g, b):
    mu = jnp.mean(y, axis=-1, keepdims=True)
    yc = y - mu
    var = jnp.mean(yc * yc, axis=-1, keepdims=True)
    return yc * lax.rsqrt(var + LN_EPS) * g + b


def _route(h2, wr_ref, rb_ref):
    logits = jnp.dot(h2, wr_ref[...], preferred_element_type=F32)
    scores = jax.nn.sigmoid(logits)
    biased = scores + rb_ref[...]
    lane = lax.broadcasted_iota(jnp.int32, biased.shape, 1)
    valid = lane < N_EXPERTS
    pos = lane & (EXPERTS_PER_GROUP - 1)

    def neighbour(kk):
        fwd = pltpu.roll(biased, LANES - kk, 1)
        bwd = pltpu.roll(biased, EXPERTS_PER_GROUP - kk, 1)
        wrapped = pos + kk >= EXPERTS_PER_GROUP
        return jnp.where(wrapped, bwd, fwd), wrapped

    (v1, w1), (v2, w2), (v3, w3) = neighbour(1), neighbour(2), neighbour(3)
    rank = jnp.zeros(biased.shape, jnp.int32)
    for v, wrapped in ((v1, w1), (v2, w2), (v3, w3)):
        ge = jnp.where(v >= biased, 1, 0)
        gt = jnp.where(v > biased, 1, 0)
        rank = rank + jnp.where(wrapped, ge, gt)
    hi_a, lo_a = jnp.maximum(biased, v1), jnp.minimum(biased, v1)
    hi_b, lo_b = jnp.maximum(v2, v3), jnp.minimum(v2, v3)
    top1 = jnp.maximum(hi_a, hi_b)
    top2 = jnp.maximum(jnp.minimum(hi_a, hi_b), jnp.maximum(lo_a, lo_b))
    grp_score = jnp.where(valid, top1 + top2, -jnp.inf)
    best = jnp.max(grp_score, axis=-1, keepdims=True)
    grp = (lane // EXPERTS_PER_GROUP).astype(F32)
    first = jnp.min(jnp.where(grp_score == best, grp, float(LANES)), axis=-1, keepdims=True)
    chosen = jnp.where(grp == first, rank, EXPERTS_PER_GROUP) < 2
    w_raw = jnp.where(chosen, scores, 0.0)
    return w_raw / jnp.sum(w_raw, axis=-1, keepdims=True)


def _post_kernel(x_ref, mod_ref, oa_ref, ob_ref, wga_ref, wgb_ref, wa_ref, wb_ref, wo_ref, lg_ref, lb_ref,
                 wr_ref, rb_ref, x1_ref, h2_ref, comb_ref):
    x = x_ref[...]
    sh1 = mod_ref[:, 0:D_MODEL]
    sc1 = mod_ref[:, D_MODEL:2 * D_MODEL]
    g1 = mod_ref[:, 2 * D_MODEL:3 * D_MODEL]
    sh2 = mod_ref[:, 3 * D_MODEL:4 * D_MODEL]
    sc2 = mod_ref[:, 4 * D_MODEL:5 * D_MODEL]
    h = (x * (1.0 + sc1) + sh1).astype(BF16)
    oa = jnp.concatenate([oa_ref[i] for i in range(H_A)], axis=1)
    ob = jnp.concatenate([ob_ref[i] for i in range(H_B)], axis=1)
    ga = jax.nn.sigmoid(jnp.dot(h, wga_ref[...], preferred_element_type=F32))
    merged = ga * jnp.dot(oa, wa_ref[...], preferred_element_type=F32)
    gb = jax.nn.sigmoid(jnp.dot(h, wgb_ref[...], preferred_element_type=F32))
    merged = merged + gb * jnp.dot(ob, wb_ref[...], preferred_element_type=F32)
    mix = jnp.dot(merged.astype(BF16), wo_ref[...], preferred_element_type=F32)
    x1 = _layer_norm(ALPHA * x + g1 * mix, lg_ref[...], lb_ref[...])
    x1_ref[...] = x1
    h2 = (x1 * (1.0 + sc2) + sh2).astype(BF16)
    h2_ref[...] = h2
    comb_ref[...] = _route(h2, wr_ref, rb_ref)


def _post(x, mod_rows, oa, ob, w_in_bf, wa_bf, wb_bf, wo_bf, ln_g, ln_b, wr_pad, rb_pad, layer, *,
          tok_per_batch, latent):
    n = x.shape[0]
    tm = TM_POST
    tiles_per_batch = tok_per_batch // tm
    const2 = lambda i: (0, 0)
    wspec = pl.BlockSpec((None, D_MODEL, D_MODEL), lambda i: (layer, 0, 0))
    return pl.pallas_call(
        _post_kernel,
        grid=(n // tm,),
        in_specs=[
            pl.BlockSpec((tm, D_MODEL), lambda i: (i, 0)),
            pl.BlockSpec((None, 1, 6 * D_MODEL), lambda i: (1 + i // tiles_per_batch if latent else 0, 0, 0)),
            pl.BlockSpec((H_A, tm, HEAD_W), lambda i: (0, i, 0)),
            pl.BlockSpec((H_B, tm, HEAD_W), lambda i: (0, i, 0)),
            pl.BlockSpec((None, D_MODEL, D_MODEL), lambda i: (layer, 0, W_QKV // D_MODEL)),
            pl.BlockSpec((None, D_MODEL, D_MODEL), lambda i: (layer, 0, W_QKV // D_MODEL + 1)),
            wspec, wspec, wspec,
            pl.BlockSpec((1, D_MODEL), const2),
            pl.BlockSpec((1, D_MODEL), const2),
            pl.BlockSpec((D_MODEL, LANES), const2),
            pl.BlockSpec((1, LANES), const2),
        ],
        out_specs=[
            pl.BlockSpec((tm, D_MODEL), lambda i: (i, 0)),
            pl.BlockSpec((tm, D_MODEL), lambda i: (i, 0)),
            pl.BlockSpec((tm, LANES), lambda i: (i, 0)),
        ],
        out_shape=[
            jax.ShapeDtypeStruct((n, D_MODEL), F32),
            jax.ShapeDtypeStruct((n, D_MODEL), BF16),
            jax.ShapeDtypeStruct((n, LANES), F32),
        ],
        compiler_params=_params(1),
        name="post_attn",
    )(x, mod_rows, oa, ob, w_in_bf, w_in_bf, wa_bf, wb_bf, wo_bf, ln_g, ln_b, wr_pad, rb_pad)


def _one_hot(idx, a, b):
    return jnp.where(idx == a, 1.0, jnp.where(idx == b, 1.0, 0.0)).astype(BF16)


def _moe_kernel(x1_ref, h2_ref, comb_ref, mod_ref, tri_ref, wg_ref, wu_ref, wd_ref, lg_ref, lb_ref, o_ref,
                hs_ref, cs_ref, ys_ref, pos_ref, meta_ref, *, tm, n_rows, unperm):
    i, e = pl.program_id(0), pl.program_id(1)
    n_alloc = hs_ref.shape[0]

    @pl.when(e == 0)
    def _():
        @pl.when(i == 0)
        def _():
            hs_ref[n_rows:n_alloc, :] = jnp.zeros((n_alloc - n_rows, D_MODEL), BF16)
            cs_ref[n_rows:n_alloc, :] = jnp.zeros((n_alloc - n_rows, LANES), F32)
            ys_ref[...] = jnp.zeros_like(ys_ref)

        comb = comb_ref[...]
        sel = comb > 0.0
        lane = lax.broadcasted_iota(jnp.int32, comb.shape, 1)
        ones = jnp.where(sel, 1.0, 0.0).astype(BF16)
        cum = jnp.dot(tri_ref[...], ones, preferred_element_type=F32)
        cnt = jnp.broadcast_to(cum[tm - 1:tm, :], (8, LANES)).astype(jnp.int32)
        padded = (cnt + (SORT_ALIGN - 1)) & -SORT_ALIGN
        lane8 = lax.broadcasted_iota(jnp.int32, padded.shape, 1)
        incl = padded
        for d in (1, 2, 4, 8):
            incl = incl + jnp.where(lane8 >= d, pltpu.roll(incl, d, 1), 0)
        off = incl - padded
        for k in range(N_EXPERTS):
            meta_ref[0, k] = off[0, k]
            meta_ref[1, k] = cnt[0, k]

        pos = off[0:1, :].astype(F32) + cum - 1.0
        p_lo = jnp.min(jnp.where(sel, pos, float(n_alloc)), axis=1, keepdims=True)
        p_hi = jnp.max(jnp.where(sel, pos, -1.0), axis=1, keepdims=True)
        pos_cols = jnp.where(lane == 0, p_lo, jnp.where(lane == 1, p_hi, 0.0))
        pos_ref[...] = pos_cols
        pos_rows = pos_cols.T
        r_lo, r_hi = pos_rows[0:1, :], pos_rows[1:2, :]

        c0 = comb.astype(BF16).astype(F32)
        c1 = (comb - c0).astype(BF16).astype(F32)
        c2 = (comb - c0 - c1).astype(BF16).astype(F32)
        pieces = (c0 + pltpu.roll(c1, N_EXPERTS, 1) + pltpu.roll(c2, 2 * N_EXPERTS, 1)).astype(BF16)
        h2 = h2_ref[...]
        for rc in range(n_rows // PERM_CHUNK):
            lo = rc * PERM_CHUNK
            ridx = (lax.broadcasted_iota(jnp.int32, (PERM_CHUNK, tm), 0) + lo).astype(F32)
            p = _one_hot(ridx, r_lo, r_hi)
            hs_ref[lo:lo + PERM_CHUNK, :] = jnp.dot(p, h2, preferred_element_type=F32).astype(BF16)
            c3 = jnp.dot(p, pieces, preferred_element_type=F32)
            cs_ref[lo:lo + PERM_CHUNK, :] = (c3 + pltpu.roll(c3, LANES - N_EXPERTS, 1)
                                             + pltpu.roll(c3, LANES - 2 * N_EXPERTS, 1))

    wg = wg_ref[...].astype(BF16)
    wu = wu_ref[...].astype(BF16)
    wd = wd_ref[...].astype(BF16)
    off_e, cnt_e = meta_ref[0, e], meta_ref[1, e]
    lane_c = lax.broadcasted_iota(jnp.int32, (MOE_CHUNK, LANES), 1)

    def chunk(j, carry):
        start = pl.multiple_of(off_e + j * MOE_CHUNK, SORT_ALIGN)
        rows = hs_ref[pl.ds(start, MOE_CHUNK), :]
        w = jnp.sum(jnp.where(lane_c == e, cs_ref[pl.ds(start, MOE_CHUNK), :], 0.0), axis=1, keepdims=True)
        g = jnp.dot(rows, wg, preferred_element_type=F32)
        u = jnp.dot(rows, wu, preferred_element_type=F32)
        act = (g * jax.nn.sigmoid(g)) * u * w
        y = jnp.dot(act.astype(BF16), wd, preferred_element_type=F32)
        ys_ref[pl.ds(start, MOE_CHUNK), :] = y.astype(BF16)
        return carry

    lax.fori_loop(0, (cnt_e + (MOE_CHUNK - 1)) // MOE_CHUNK, chunk, 0)

    @pl.when(e == N_EXPERTS - 1)
    def _():
        p_lo, p_hi = pos_ref[:, 0:1], pos_ref[:, 1:2]
        y = None
        for cc in range(n_rows // unperm):
            lo = cc * unperm
            cidx = (lax.broadcasted_iota(jnp.int32, (tm, unperm), 1) + lo).astype(F32)
            q = _one_hot(cidx, p_lo, p_hi)
            part = jnp.dot(q, ys_ref[lo:lo + unperm, :], preferred_element_type=F32)
            y = part if y is None else y + part
        g2 = mod_ref[:, 5 * D_MODEL:6 * D_MODEL]
        o_ref[...] = _layer_norm(ALPHA * x1_ref[...] + g2 * y, lg_ref[...], lb_ref[...])


def _moe(x1, h2, comb, mod_rows, w_gate, w_up, w_down, ln_g, ln_b, layer, *, tok_per_batch, latent):
    n = x1.shape[0]
    tm = min(TM_MOE, tok_per_batch if latent else n)
    tiles_per_batch = max(tok_per_batch // tm, 1)
    n_rows = TOP_K * tm + N_EXPERTS * SORT_ALIGN
    assert n_rows % PERM_CHUNK == 0
    unperm = math.gcd(n_rows, UNPERM_CHUNK)
    n_alloc = n_rows + MOE_CHUNK
    tri = jnp.tril(jnp.ones((tm, tm), BF16))
    const2 = lambda i, e: (0, 0)
    once = dict(pipeline_mode=pl.Buffered(1))
    return pl.pallas_call(
        functools.partial(_moe_kernel, tm=tm, n_rows=n_rows, unperm=unperm),
        grid=(n // tm, N_EXPERTS),
        in_specs=[
            pl.BlockSpec((tm, D_MODEL), lambda i, e: (i, 0), **once),
            pl.BlockSpec((tm, D_MODEL), lambda i, e: (i, 0), **once),
            pl.BlockSpec((tm, LANES), lambda i, e: (i, 0), **once),
            pl.BlockSpec((None, 1, 6 * D_MODEL),
                         lambda i, e: (1 + i // tiles_per_batch if latent else 0, 0, 0)),
            pl.BlockSpec((tm, tm), const2, **once),
            pl.BlockSpec((None, None, D_MODEL, D_EXPERT), lambda i, e: (layer, e, 0, 0)),
            pl.BlockSpec((None, None, D_MODEL, D_EXPERT), lambda i, e: (layer, e, 0, 0)),
            pl.BlockSpec((None, None, D_EXPERT, D_MODEL), lambda i, e: (layer, e, 0, 0)),
            pl.BlockSpec((1, D_MODEL), const2),
            pl.BlockSpec((1, D_MODEL), const2),
        ],
        out_specs=pl.BlockSpec((tm, D_MODEL), lambda i, e: (i, 0)),
        out_shape=jax.ShapeDtypeStruct((n, D_MODEL), F32),
        scratch_shapes=[
            pltpu.VMEM((n_alloc, D_MODEL), BF16),
            pltpu.VMEM((n_alloc, LANES), F32),
            pltpu.VMEM((n_alloc, D_MODEL), BF16),
            pltpu.VMEM((tm, LANES), F32),
            pltpu.SMEM((2, N_EXPERTS), jnp.int32),
        ],
        compiler_params=_params(2),
        name="moe_ln2",
    )(x1, h2, comb, mod_rows, tri, w_gate, w_up, w_down, ln_g, ln_b)


def _lambda_init(layer):
    return 0.8 - 0.6 * math.exp(-0.3 * layer)


def kernel(x_prompt, x_sample, cache_a_k, cache_a_v, cache_b_k, cache_b_v, c, c_ctx, w_mod, b_mod, w_in,
           lam_q1, lam_k1, lam_q2, lam_k2, subln_g, qn_g, kn_g, w_br_a, w_br_b, w_out, ln1_g, ln1_b,
           ln2_g, ln2_b, w_router, router_bias, w_gate, w_up, w_down):
    batch, seq, _ = x_prompt.shape
    dec_batch, dec_seq, _ = x_sample.shape

    cond8 = jnp.zeros((8, D_MODEL), F32).at[0].set(c_ctx).at[1:1 + dec_batch].set(c)
    mod = _modulation(cond8, w_mod, b_mod).reshape(DEPTH, 8, 1, 6 * D_MODEL)

    w_in_bf = w_in.astype(BF16)
    wa_bf, wb_bf, wo_bf = w_br_a.astype(BF16), w_br_b.astype(BF16), w_out.astype(BF16)
    wr_pad = jnp.zeros((D_MODEL, LANES), BF16).at[:, :N_EXPERTS].set(w_router.astype(BF16))
    rb_pad = jnp.zeros((1, LANES), F32).at[0, :N_EXPERTS].set(router_bias)
    rope_tabs = _rope_tables(dec_seq, D_A) + _rope_tables(dec_seq, D_B)
    row = lambda a, l: a[l].reshape(1, -1)
    cak, cav, cbk, cbv = (a.reshape(*a.shape[:3], -1) for a in (cache_a_k, cache_a_v, cache_b_k, cache_b_v))

    def layer_step(x, l, state_bufs, latent):
        tok = dec_seq if latent else seq
        nb = dec_batch if latent else batch
        outs = _qkv(x, mod[l], w_in_bf, row(qn_g, l), row(kn_g, l), l, tok_per_batch=tok,
                    rope_tabs=rope_tabs if latent else None, state_bufs=state_bufs)
        qa, ka, va, qb, kb, vb = outs[:6]
        tq = min(TQ_ATTN, tok)
        oa = _attention(qa, ka, va, diff=True, n_batch=nb, tok_per_batch=tok, tq=tq, layer=l,
                        lam_init=_lambda_init(l), subln_g=row(subln_g, l),
                        lam_params=tuple(row(a, l) for a in (lam_q1, lam_k1, lam_q2, lam_k2)),
                        cache_k=cak if latent else None, cache_v=cav if latent else None)
        ob = _attention(qb, kb, vb, diff=False, n_batch=nb, tok_per_batch=tok, tq=tq, layer=l,
                        cache_k=cbk if latent else None, cache_v=cbv if latent else None)
        x1, h2, comb = _post(x, mod[l], oa, ob, w_in_bf, wa_bf, wb_bf, wo_bf, row(ln1_g, l), row(ln1_b, l),
                             wr_pad, rb_pad, l, tok_per_batch=tok, latent=latent)
        x2 = _moe(x1, h2, comb, mod[l], w_gate, w_up, w_down, row(ln2_g, l), row(ln2_b, l), l,
                  tok_per_batch=tok, latent=latent)
        return x2, (outs[6:] if state_bufs is not None else None)

    wa_state = H_A * HEAD_W
    wb_state = G_B * HEAD_W
    state = tuple(jnp.zeros((batch, DEPTH, seq, w), F32) for w in (wa_state, wa_state, wb_state, wb_state))
    y = x_prompt.reshape(batch * seq, D_MODEL)
    for l in range(DEPTH):
        y, state = layer_step(y, l, state, latent=False)
    y_prompt = y.reshape(batch, seq, D_MODEL)

    y = x_sample.reshape(dec_batch * dec_seq, D_MODEL)
    for l in range(DEPTH):
        y, _ = layer_step(y, l, None, latent=True)
    y_sample = y.reshape(dec_batch, dec_seq, D_MODEL)

    sak, sav, sbk, sbv = state
    return (y_prompt, y_sample,
            sak.reshape(batch, DEPTH, seq, H_A, 2 * D_A), sav.reshape(batch, DEPTH, seq, H_A, 2 * D_A),
            sbk.reshape(batch, DEPTH, seq, G_B, D_B), sbv.reshape(batch, DEPTH, seq, G_B, D_B))
```

```python
import functools
import math

import jax
import jax.numpy as jnp
from jax import lax
from jax.experimental import pallas as pl
from jax.experimental.pallas import tpu as pltpu

D_MODEL = 1024
DEPTH = 4
GRID_W = 64
ROPE_THETA = 10000.0
H_A = 8
D_A = 64
H_B = 8
G_B = 4
D_B = 128
HEAD_W = 128
N_EXPERTS = 16
EXPERTS_PER_GROUP = 4
D_EXPERT = 512
ALPHA = (2 * DEPTH) ** 0.25
LN_EPS = 1e-6
RMS_EPS = 1e-6
W_QKV = 3 * H_A * 2 * D_A + H_B * D_B + 2 * G_B * D_B
LOG2E = 1.4426950408889634
LANES = 128
VMEM_LIMIT = 56 * 1024 * 1024

BF16 = jnp.bfloat16
F32 = jnp.float32

TM_QKV = 256
TM_POST = 256
TM_MOE = 1024
TN_MOD = 1536
TOP_K = 2
SORT_ALIGN = 16
MOE_CHUNK = 160
PERM_CHUNK = 256
UNPERM_CHUNK = 768
TQ_ATTN = 1024
KEY_CHUNK = 512


def _params(n_axes, vmem=VMEM_LIMIT):
    return pltpu.CompilerParams(dimension_semantics=("arbitrary",) * n_axes,
                                vmem_limit_bytes=vmem)


def _mod_kernel(c_ref, w_ref, b_ref, o_ref):
    c = c_ref[...]
    s = (c * jax.nn.sigmoid(c)).astype(BF16)
    o_ref[...] = jnp.dot(s, w_ref[...].astype(BF16), preferred_element_type=F32) + b_ref[...]


def _modulation(cond8, w_mod, b_mod):
    n_out = 6 * D_MODEL
    return pl.pallas_call(
        _mod_kernel,
        grid=(DEPTH, n_out // TN_MOD),
        in_specs=[
            pl.BlockSpec((8, D_MODEL), lambda l, j: (0, 0)),
            pl.BlockSpec((None, D_MODEL, TN_MOD), lambda l, j: (l, 0, j)),
            pl.BlockSpec((None, 1, TN_MOD), lambda l, j: (l, 0, j)),
        ],
        out_specs=pl.BlockSpec((None, 8, TN_MOD), lambda l, j: (l, 0, j)),
        out_shape=jax.ShapeDtypeStruct((DEPTH, 8, n_out), F32),
        compiler_params=_params(2),
        name="modulation",
    )(cond8, w_mod, b_mod.reshape(DEPTH, 1, n_out))


def _rope_tables(n_tok, d):
    nf = d // 4
    n_rows = n_tok // GRID_W
    row = jnp.repeat(jnp.arange(n_rows, dtype=F32), GRID_W)
    col = jnp.tile(jnp.arange(GRID_W, dtype=F32), n_rows)
    inv = ROPE_THETA ** (-jnp.arange(nf, dtype=F32) / nf)
    lane = jnp.arange(LANES) % d
    axis = lane // (2 * nf)
    half = (lane % (2 * nf)) // nf
    freq = lane % nf
    pos = jnp.where(axis[None, :] == 0, row[:, None], col[:, None])
    ang = pos * inv[freq][None, :]
    cos, sin = jnp.cos(ang), jnp.sin(ang)
    sin_lo = jnp.where(half[None, :] == 0, -sin, 0.0)
    sin_hi = jnp.where(half[None, :] == 1, sin, 0.0)
    return cos, sin_lo, sin_hi


def _rope(x, cos, sin_lo, sin_hi, nf):
    up = pltpu.roll(x, LANES - nf, 1)
    down = pltpu.roll(x, nf, 1)
    return x * cos + up * sin_lo + down * sin_hi


def _rms(x, g):
    ms = jnp.mean(x * x, axis=-1, keepdims=True)
    return x * lax.rsqrt(ms + RMS_EPS) * g


def _qkv_kernel(*refs, rope, states):
    x_ref, mod_ref, w_ref, qn_ref, kn_ref = refs[:5]
    pos = 5
    if rope:
        ca, sa_lo, sa_hi, cb, sb_lo, sb_hi = (r[...] for r in refs[pos:pos + 6])
        pos += 6
    if states:
        pos += 4
    qa_ref, ka_ref, va_ref, qb_ref, kb_ref, vb_ref = refs[pos:pos + 6]
    pos += 6
    if states:
        sak_ref, sav_ref, sbk_ref, sbv_ref = refs[pos:pos + 4]

    sh1 = mod_ref[:, 0:D_MODEL]
    sc1 = mod_ref[:, D_MODEL:2 * D_MODEL]
    h = (x_ref[...] * (1.0 + sc1) + sh1).astype(BF16)

    def seg(lo, width):
        return jnp.dot(h, w_ref[:, lo:lo + width], preferred_element_type=F32)

    def heads(p, n):
        return [p[:, i * HEAD_W:(i + 1) * HEAD_W] for i in range(n)]

    qa_scale = (D_A ** -0.5) * LOG2E
    qb_scale = (D_B ** -0.5) * LOG2E
    nfa, nfb = D_A // 4, D_B // 4

    p = seg(0, H_A * HEAD_W)
    for i, c in enumerate(heads(p, H_A)):
        if rope:
            c = _rope(c, ca, sa_lo, sa_hi, nfa)
        qa_ref[i] = (c * qa_scale).astype(BF16)

    p = seg(H_A * HEAD_W, H_A * HEAD_W)
    if states:
        sak_ref[...] = p
    for i, c in enumerate(heads(p, H_A)):
        if rope:
            c = _rope(c, ca, sa_lo, sa_hi, nfa)
        ka_ref[i] = c.astype(BF16)

    p = seg(2 * H_A * HEAD_W, H_A * HEAD_W)
    if states:
        sav_ref[...] = p
    for i, c in enumerate(heads(p, H_A)):
        va_ref[i] = c.astype(BF16)

    off = 3 * H_A * HEAD_W
    p = seg(off, H_B * HEAD_W)
    qn = qn_ref[...]
    for i, c in enumerate(heads(p, H_B)):
        c = _rms(c, qn)
        if rope:
            c = _rope(c, cb, sb_lo, sb_hi, nfb)
        qb_ref[i] = (c * qb_scale).astype(BF16)

    off += H_B * HEAD_W
    p = seg(off, G_B * HEAD_W)
    kn = kn_ref[...]
    for i, c in enumerate(heads(p, G_B)):
        c = _rms(c, kn)
        if states:
            sbk_ref[:, i * HEAD_W:(i + 1) * HEAD_W] = c
        if rope:
            c = _rope(c, cb, sb_lo, sb_hi, nfb)
        kb_ref[i] = c.astype(BF16)

    off += G_B * HEAD_W
    p = seg(off, G_B * HEAD_W)
    if states:
        sbv_ref[...] = p
    for i, c in enumerate(heads(p, G_B)):
        vb_ref[i] = c.astype(BF16)


def _qkv(x, mod_rows, w_in_bf, qn_g, kn_g, layer, *, tok_per_batch, rope_tabs=None, state_bufs=None):
    n = x.shape[0]
    tm = TM_QKV
    tiles_per_batch = tok_per_batch // tm
    rope = rope_tabs is not None
    states = state_bufs is not None
    row0 = 1 if rope else 0

    in_specs = [
        pl.BlockSpec((tm, D_MODEL), lambda i: (i, 0)),
        pl.BlockSpec((None, 1, 6 * D_MODEL), lambda i: (row0 + i // tiles_per_batch if rope else 0, 0, 0)),
        pl.BlockSpec((None, D_MODEL, W_QKV), lambda i: (layer, 0, 0)),
        pl.BlockSpec((1, HEAD_W), lambda i: (0, 0)),
        pl.BlockSpec((1, HEAD_W), lambda i: (0, 0)),
    ]
    args = [x, mod_rows, w_in_bf, qn_g, kn_g]
    if rope:
        in_specs += [pl.BlockSpec((tm, LANES), lambda i: (i % tiles_per_batch, 0))] * 6
        args += list(rope_tabs)
    aliases = {}
    if states:
        for k, buf in enumerate(state_bufs):
            aliases[len(args)] = 6 + k
            in_specs.append(pl.BlockSpec(memory_space=pl.ANY))
            args.append(buf)

    def hm(nh):
        return (pl.BlockSpec((nh, tm, HEAD_W), lambda i: (0, i, 0)),
                jax.ShapeDtypeStruct((nh, n, HEAD_W), BF16))

    outs = [hm(H_A), hm(H_A), hm(H_A), hm(H_B), hm(G_B), hm(G_B)]
    if states:
        assert tm == tok_per_batch
        for buf in state_bufs:
            outs.append((pl.BlockSpec((None, None, tm, buf.shape[-1]), lambda i: (i, layer, 0, 0)),
                         jax.ShapeDtypeStruct(buf.shape, F32)))
    return pl.pallas_call(
        functools.partial(_qkv_kernel, rope=rope, states=states),
        grid=(n // tm,),
        in_specs=in_specs,
        out_specs=[o[0] for o in outs],
        out_shape=[o[1] for o in outs],
        input_output_aliases=aliases,
        compiler_params=_params(1),
        name="qkv_states" if states else "qkv_rope",
    )(*args)


def _attn_kernel(*refs, diff, n_cache, n_new, tq, kc, hb, lam_init):
    if diff:
        lq1, lk1, lq2, lk2, g_ref = refs[:5]
        refs = refs[5:]
    q_ref, kn_ref, vn_ref = refs[:3]
    refs = refs[3:]
    if n_cache:
        kc_ref, vc_ref = refs[:2]
        refs = refs[2:]
    o_ref = refs[0]
    n_keys = n_cache + n_new

    def transposed(x):
        return x.astype(F32).T.astype(BF16)

    if n_cache:
        assert hb == 1
        k_all, vt_all = refs[1:3]

        @pl.when(pl.program_id(2) == 0)
        def _():
            k_all[0:n_cache, :] = kc_ref[...].astype(BF16)
            k_all[n_cache:n_keys, :] = kn_ref[0]
            vt_all[:, 0:n_cache] = transposed(vc_ref[...])
            vt_all[:, n_cache:n_keys] = transposed(vn_ref[0])

        k_chunk = lambda j, c: k_all[c * kc:(c + 1) * kc, :]
        vt_chunk = lambda j, c: vt_all[:, c * kc:(c + 1) * kc]
    else:
        vts = [transposed(vn_ref[j]) for j in range(hb)]
        k_chunk = lambda j, c: kn_ref[j, c * kc:(c + 1) * kc, :]
        vt_chunk = lambda j, c: vts[j][:, c * kc:(c + 1) * kc]

    def q_columns(j):
        if diff:
            qt = q_ref[j].astype(F32).T
            row = lax.broadcasted_iota(jnp.int32, qt.shape, 0)
            qt = jnp.concatenate([jnp.where(row < D_A, qt, 0.0), jnp.where(row >= D_A, qt, 0.0)], axis=1)
        else:
            qt = jnp.concatenate([q_ref[2 * j].astype(F32).T, q_ref[2 * j + 1].astype(F32).T], axis=1)
        return qt.astype(BF16)

    qts = [q_columns(j) for j in range(hb)]

    def scores(item):
        j, c = item
        return jnp.dot(k_chunk(j, c), qts[j], preferred_element_type=F32)

    if diff:
        lam = (jnp.exp(jnp.sum(lq1[...] * lk1[...], axis=-1, keepdims=True))
               - jnp.exp(jnp.sum(lq2[...] * lk2[...], axis=-1, keepdims=True)) + lam_init)

    def finish(j, l, acc):
        inv = 1.0 / l
        if diff:
            o = acc[:, :tq] * inv[:, :tq] - acc[:, tq:] * (lam * inv[:, tq:])
            o = _rms(o.T, g_ref[...]) * (1.0 - lam_init)
            o_ref[j] = o.astype(BF16)
        else:
            o = acc * inv
            o_ref[2 * j] = o[:, :tq].T.astype(BF16)
            o_ref[2 * j + 1] = o[:, tq:].T.astype(BF16)

    n_chunks = n_keys // kc
    items = [(j, c) for j in range(hb) for c in range(n_chunks)]
    m = l = acc = None
    s_next = scores(items[0])
    for idx, (j, c) in enumerate(items):
        s = s_next
        if idx + 1 < len(items):
            s_next = scores(items[idx + 1])
        m_c = jnp.max(s, axis=0, keepdims=True)
        m_new = m_c if c == 0 else jnp.maximum(m, m_c)
        e = jnp.exp2(s - m_new)
        l_c = jnp.sum(e, axis=0, keepdims=True)
        pv = jnp.dot(vt_chunk(j, c), e.astype(BF16), preferred_element_type=F32)
        if c == 0:
            l, acc = l_c, pv
        else:
            alpha = jnp.exp2(m - m_new)
            l = alpha * l + l_c
            acc = alpha * acc + pv
        m = m_new
        if c == n_chunks - 1:
            finish(j, l, acc)


def _attention(q, k_new, v_new, *, diff, n_batch, tok_per_batch, tq, layer, lam_init=0.0,
               lam_params=None, subln_g=None, cache_k=None, cache_v=None):
    n_kv = k_new.shape[0]
    n = q.shape[1]
    nq = tok_per_batch // tq
    n_cache = 0 if cache_k is None else cache_k.shape[2]
    n_keys = n_cache + tok_per_batch
    kc = math.gcd(n_keys, KEY_CHUNK)
    hb = 1 if n_cache else n_kv

    in_specs, args = [], []
    if diff:
        in_specs += [pl.BlockSpec((1, D_A), lambda b, h, i: (0, 0))] * 4
        args += list(lam_params)
        in_specs.append(pl.BlockSpec((1, HEAD_W), lambda b, h, i: (0, 0)))
        args.append(subln_g)
    q_block = (hb if diff else 2 * hb, tq, HEAD_W)
    in_specs += [
        pl.BlockSpec(q_block, lambda b, h, i: (h, b * nq + i, 0)),
        pl.BlockSpec((hb, tok_per_batch, HEAD_W), lambda b, h, i: (h, b, 0)),
        pl.BlockSpec((hb, tok_per_batch, HEAD_W), lambda b, h, i: (h, b, 0)),
    ]
    args += [q, k_new, v_new]
    scratch = []
    if n_cache:
        cspec = pl.BlockSpec((None, None, n_cache, HEAD_W), lambda b, h, i: (b, layer, 0, h))
        in_specs += [cspec, cspec]
        args += [cache_k, cache_v]
        scratch = [pltpu.VMEM((n_keys, HEAD_W), BF16), pltpu.VMEM((HEAD_W, n_keys), BF16)]
    return pl.pallas_call(
        functools.partial(_attn_kernel, diff=diff, n_cache=n_cache, n_new=tok_per_batch, tq=tq, kc=kc,
                          hb=hb, lam_init=lam_init),
        grid=(n_batch, n_kv // hb, nq),
        in_specs=in_specs,
        out_specs=pl.BlockSpec(q_block, lambda b, h, i: (h, b * nq + i, 0)),
        out_shape=jax.ShapeDtypeStruct((H_A, n, HEAD_W), BF16),
        scratch_shapes=scratch,
        compiler_params=_params(3),
        name=("attn_diff" if diff else "attn_gqa") + ("_cache" if n_cache else ""),
    )(*args)


def _layer_norm(y, g, b):
    mu = jnp.mean(y, axis=-1, keepdims=True)
    yc = y - mu
    var = jnp.mean(yc * yc, axis=-1, keepdims=True)
    return yc * lax.rsqrt(var + LN_EPS) * g + b


def _route(h2, wr_ref, rb_ref):
    logits = jnp.dot(h2, wr_ref[...], preferred_element_type=F32)
    scores = jax.nn.sigmoid(logits)
    biased = scores + rb_ref[...]
    lane = lax.broadcasted_iota(jnp.int32, biased.shape, 1)
    valid = lane < N_EXPERTS
    pos = lane & (EXPERTS_PER_GROUP - 1)

    def neighbour(kk):
        fwd = pltpu.roll(biased, LANES - kk, 1)
        bwd = pltpu.roll(biased, EXPERTS_PER_GROUP - kk, 1)
        wrapped = pos + kk >= EXPERTS_PER_GROUP
        return jnp.where(wrapped, bwd, fwd), wrapped

    (v1, w1), (v2, w2), (v3, w3) = neighbour(1), neighbour(2), neighbour(3)
    rank = jnp.zeros(biased.shape, jnp.int32)
    for v, wrapped in ((v1, w1), (v2, w2), (v3, w3)):
        ge = jnp.where(v >= biased, 1, 0)
        gt = jnp.where(v > biased, 1, 0)
        rank = rank + jnp.where(wrapped, ge, gt)
    hi_a, lo_a = jnp.maximum(biased, v1), jnp.minimum(biased, v1)
    hi_b, lo_b = jnp.maximum(v2, v3), jnp.minimum(v2, v3)
    top1 = jnp.maximum(hi_a, hi_b)
    top2 = jnp.maximum(jnp.minimum(hi_a, hi_b), jnp.maximum(lo_a, lo_b))
    grp_score = jnp.where(valid, top1 + top2, -jnp.inf)
    best = jnp.max(grp_score, axis=-1, keepdims=True)
    grp = (lane // EXPERTS_PER_GROUP).astype(F32)
    first = jnp.min(jnp.where(grp_score == best, grp, float(LANES)), axis=-1, keepdims=True)
    chosen = jnp.where(grp == first, rank, EXPERTS_PER_GROUP) < 2
    w_raw = jnp.where(chosen, scores, 0.0)
    return w_raw / jnp.sum(w_raw, axis=-1, keepdims=True)


def _post_kernel(x_ref, mod_ref, oa_ref, ob_ref, wga_ref, wgb_ref, wa_ref, wb_ref, wo_ref, lg_ref, lb_ref,
                 wr_ref, rb_ref, x1_ref, h2_ref, comb_ref):
    x = x_ref[...]
    sh1 = mod_ref[:, 0:D_MODEL]
    sc1 = mod_ref[:, D_MODEL:2 * D_MODEL]
    g1 = mod_ref[:, 2 * D_MODEL:3 * D_MODEL]
    sh2 = mod_ref[:, 3 * D_MODEL:4 * D_MODEL]
    sc2 = mod_ref[:, 4 * D_MODEL:5 * D_MODEL]
    h = (x * (1.0 + sc1) + sh1).astype(BF16)
    oa = jnp.concatenate([oa_ref[i] for i in range(H_A)], axis=1)
    ob = jnp.concatenate([ob_ref[i] for i in range(H_B)], axis=1)
    ga = jax.nn.sigmoid(jnp.dot(h, wga_ref[...], preferred_element_type=F32))
    merged = ga * jnp.dot(oa, wa_ref[...], preferred_element_type=F32)
    gb = jax.nn.sigmoid(jnp.dot(h, wgb_ref[...], preferred_element_type=F32))
    merged = merged + gb * jnp.dot(ob, wb_ref[...], preferred_element_type=F32)
    mix = jnp.dot(merged.astype(BF16), wo_ref[...], preferred_element_type=F32)
    x1 = _layer_norm(ALPHA * x + g1 * mix, lg_ref[...], lb_ref[...])
    x1_ref[...] = x1
    h2 = (x1 * (1.0 + sc2) + sh2).astype(BF16)
    h2_ref[...] = h2
    comb_ref[...] = _route(h2, wr_ref, rb_ref)


def _post(x, mod_rows, oa, ob, w_in_bf, wa_bf, wb_bf, wo_bf, ln_g, ln_b, wr_pad, rb_pad, layer, *,
          tok_per_batch, latent):
    n = x.shape[0]
    tm = TM_POST
    tiles_per_batch = tok_per_batch // tm
    const2 = lambda i: (0, 0)
    wspec = pl.BlockSpec((None, D_MODEL, D_MODEL), lambda i: (layer, 0, 0))
    return pl.pallas_call(
        _post_kernel,
        grid=(n // tm,),
        in_specs=[
            pl.BlockSpec((tm, D_MODEL), lambda i: (i, 0)),
            pl.BlockSpec((None, 1, 6 * D_MODEL), lambda i: (1 + i // tiles_per_batch if latent else 0, 0, 0)),
            pl.BlockSpec((H_A, tm, HEAD_W), lambda i: (0, i, 0)),
            pl.BlockSpec((H_B, tm, HEAD_W), lambda i: (0, i, 0)),
            pl.BlockSpec((None, D_MODEL, D_MODEL), lambda i: (layer, 0, W_QKV // D_MODEL)),
            pl.BlockSpec((None, D_MODEL, D_MODEL), lambda i: (layer, 0, W_QKV // D_MODEL + 1)),
            wspec, wspec, wspec,
            pl.BlockSpec((1, D_MODEL), const2),
            pl.BlockSpec((1, D_MODEL), const2),
            pl.BlockSpec((D_MODEL, LANES), const2),
            pl.BlockSpec((1, LANES), const2),
        ],
        out_specs=[
            pl.BlockSpec((tm, D_MODEL), lambda i: (i, 0)),
            pl.BlockSpec((tm, D_MODEL), lambda i: (i, 0)),
            pl.BlockSpec((tm, LANES), lambda i: (i, 0)),
        ],
        out_shape=[
            jax.ShapeDtypeStruct((n, D_MODEL), F32),
            jax.ShapeDtypeStruct((n, D_MODEL), BF16),
            jax.ShapeDtypeStruct((n, LANES), F32),
        ],
        compiler_params=_params(1),
        name="post_attn",
    )(x, mod_rows, oa, ob, w_in_bf, w_in_bf, wa_bf, wb_bf, wo_bf, ln_g, ln_b, wr_pad, rb_pad)


def _one_hot(idx, a, b):
    return jnp.where(idx == a, 1.0, jnp.where(idx == b, 1.0, 0.0)).astype(BF16)


def _moe_kernel(x1_ref, h2_ref, comb_ref, mod_ref, tri_ref, wg_ref, wu_ref, wd_ref, lg_ref, lb_ref, o_ref,
                hs_ref, cs_ref, ys_ref, pos_ref, meta_ref, *, tm, n_rows, unperm):
    i, e = pl.program_id(0), pl.program_id(1)
    n_alloc = hs_ref.shape[0]

    @pl.when(e == 0)
    def _():
        @pl.when(i == 0)
        def _():
            hs_ref[n_rows:n_alloc, :] = jnp.zeros((n_alloc - n_rows, D_MODEL), BF16)
            cs_ref[n_rows:n_alloc, :] = jnp.zeros((n_alloc - n_rows, LANES), F32)
            ys_ref[...] = jnp.zeros_like(ys_ref)

        comb = comb_ref[...]
        sel = comb > 0.0
        lane = lax.broadcasted_iota(jnp.int32, comb.shape, 1)
        ones = jnp.where(sel, 1.0, 0.0).astype(BF16)
        cum = jnp.dot(tri_ref[...], ones, preferred_element_type=F32)
        cnt = jnp.broadcast_to(cum[tm - 1:tm, :], (8, LANES)).astype(jnp.int32)
        padded = (cnt + (SORT_ALIGN - 1)) & -SORT_ALIGN
        lane8 = lax.broadcasted_iota(jnp.int32, padded.shape, 1)
        incl = padded
        for d in (1, 2, 4, 8):
            incl = incl + jnp.where(lane8 >= d, pltpu.roll(incl, d, 1), 0)
        off = incl - padded
        for k in range(N_EXPERTS):
            meta_ref[0, k] = off[0, k]
            meta_ref[1, k] = cnt[0, k]

        pos = off[0:1, :].astype(F32) + cum - 1.0
        p_lo = jnp.min(jnp.where(sel, pos, float(n_alloc)), axis=1, keepdims=True)
        p_hi = jnp.max(jnp.where(sel, pos, -1.0), axis=1, keepdims=True)
        pos_cols = jnp.where(lane == 0, p_lo, jnp.where(lane == 1, p_hi, 0.0))
        pos_ref[...] = pos_cols
        pos_rows = pos_cols.T
        r_lo, r_hi = pos_rows[0:1, :], pos_rows[1:2, :]

        c0 = comb.astype(BF16).astype(F32)
        c1 = (comb - c0).astype(BF16).astype(F32)
        c2 = (comb - c0 - c1).astype(BF16).astype(F32)
        pieces = (c0 + pltpu.roll(c1, N_EXPERTS, 1) + pltpu.roll(c2, 2 * N_EXPERTS, 1)).astype(BF16)
        h2 = h2_ref[...]
        for rc in range(n_rows // PERM_CHUNK):
            lo = rc * PERM_CHUNK
            ridx = (lax.broadcasted_iota(jnp.int32, (PERM_CHUNK, tm), 0) + lo).astype(F32)
            p = _one_hot(ridx, r_lo, r_hi)
            hs_ref[lo:lo + PERM_CHUNK, :] = jnp.dot(p, h2, preferred_element_type=F32).astype(BF16)
            c3 = jnp.dot(p, pieces, preferred_element_type=F32)
            cs_ref[lo:lo + PERM_CHUNK, :] = (c3 + pltpu.roll(c3, LANES - N_EXPERTS, 1)
                                             + pltpu.roll(c3, LANES - 2 * N_EXPERTS, 1))

    wg, wu, wd = wg_ref[...], wu_ref[...], wd_ref[...]
    off_e, cnt_e = meta_ref[0, e], meta_ref[1, e]
    lane_c = lax.broadcasted_iota(jnp.int32, (MOE_CHUNK, LANES), 1)

    def chunk(j, carry):
        start = pl.multiple_of(off_e + j * MOE_CHUNK, SORT_ALIGN)
        rows = hs_ref[pl.ds(start, MOE_CHUNK), :]
        w = jnp.sum(jnp.where(lane_c == e, cs_ref[pl.ds(start, MOE_CHUNK), :], 0.0), axis=1, keepdims=True)
        g = jnp.dot(rows, wg, preferred_element_type=F32)
        u = jnp.dot(rows, wu, preferred_element_type=F32)
        act = (g * jax.nn.sigmoid(g)) * u * w
        y = jnp.dot(act.astype(BF16), wd, preferred_element_type=F32)
        ys_ref[pl.ds(start, MOE_CHUNK), :] = y.astype(BF16)
        return carry

    lax.fori_loop(0, (cnt_e + (MOE_CHUNK - 1)) // MOE_CHUNK, chunk, 0)

    @pl.when(e == N_EXPERTS - 1)
    def _():
        p_lo, p_hi = pos_ref[:, 0:1], pos_ref[:, 1:2]
        y = None
        for cc in range(n_rows // unperm):
            lo = cc * unperm
            cidx = (lax.broadcasted_iota(jnp.int32, (tm, unperm), 1) + lo).astype(F32)
            q = _one_hot(cidx, p_lo, p_hi)
            part = jnp.dot(q, ys_ref[lo:lo + unperm, :], preferred_element_type=F32)
            y = part if y is None else y + part
        g2 = mod_ref[:, 5 * D_MODEL:6 * D_MODEL]
        o_ref[...] = _layer_norm(ALPHA * x1_ref[...] + g2 * y, lg_ref[...], lb_ref[...])


def _moe(x1, h2, comb, mod_rows, w_gate, w_up, w_down, ln_g, ln_b, layer, *, tok_per_batch, latent):
    n = x1.shape[0]
    tm = min(TM_MOE, tok_per_batch if latent else n)
    tiles_per_batch = max(tok_per_batch // tm, 1)
    n_rows = TOP_K * tm + N_EXPERTS * SORT_ALIGN
    assert n_rows % PERM_CHUNK == 0
    unperm = math.gcd(n_rows, UNPERM_CHUNK)
    n_alloc = n_rows + MOE_CHUNK
    tri = jnp.tril(jnp.ones((tm, tm), BF16))
    const2 = lambda i, e: (0, 0)
    once = dict(pipeline_mode=pl.Buffered(1))
    return pl.pallas_call(
        functools.partial(_moe_kernel, tm=tm, n_rows=n_rows, unperm=unperm),
        grid=(n // tm, N_EXPERTS),
        in_specs=[
            pl.BlockSpec((tm, D_MODEL), lambda i, e: (i, 0), **once),
            pl.BlockSpec((tm, D_MODEL), lambda i, e: (i, 0), **once),
            pl.BlockSpec((tm, LANES), lambda i, e: (i, 0), **once),
            pl.BlockSpec((None, 1, 6 * D_MODEL),
                         lambda i, e: (1 + i // tiles_per_batch if latent else 0, 0, 0)),
            pl.BlockSpec((tm, tm), const2, **once),
            pl.BlockSpec((None, None, D_MODEL, D_EXPERT), lambda i, e: (layer, e, 0, 0)),
            pl.BlockSpec((None, None, D_MODEL, D_EXPERT), lambda i, e: (layer, e, 0, 0)),
            pl.BlockSpec((None, None, D_EXPERT, D_MODEL), lambda i, e: (layer, e, 0, 0)),
            pl.BlockSpec((1, D_MODEL), const2),
            pl.BlockSpec((1, D_MODEL), const2),
        ],
        out_specs=pl.BlockSpec((tm, D_MODEL), lambda i, e: (i, 0)),
        out_shape=jax.ShapeDtypeStruct((n, D_MODEL), F32),
        scratch_shapes=[
            pltpu.VMEM((n_alloc, D_MODEL), BF16),
            pltpu.VMEM((n_alloc, LANES), F32),
            pltpu.VMEM((n_alloc, D_MODEL), BF16),
            pltpu.VMEM((tm, LANES), F32),
            pltpu.SMEM((2, N_EXPERTS), jnp.int32),
        ],
        compiler_params=_params(2),
        name="moe_ln2",
    )(x1, h2, comb, mod_rows, tri, w_gate, w_up, w_down, ln_g, ln_b)


def _lambda_init(layer):
    return 0.8 - 0.6 * math.exp(-0.3 * layer)


def kernel(x_prompt, x_sample, cache_a_k, cache_a_v, cache_b_k, cache_b_v, c, c_ctx, w_mod, b_mod, w_in,
           lam_q1, lam_k1, lam_q2, lam_k2, subln_g, qn_g, kn_g, w_br_a, w_br_b, w_out, ln1_g, ln1_b,
           ln2_g, ln2_b, w_router, router_bias, w_gate, w_up, w_down):
    batch, seq, _ = x_prompt.shape
    dec_batch, dec_seq, _ = x_sample.shape

    cond8 = jnp.zeros((8, D_MODEL), F32).at[0].set(c_ctx).at[1:1 + dec_batch].set(c)
    mod = _modulation(cond8, w_mod, b_mod).reshape(DEPTH, 8, 1, 6 * D_MODEL)

    w_in_bf = w_in.astype(BF16)
    wa_bf, wb_bf, wo_bf = w_br_a.astype(BF16), w_br_b.astype(BF16), w_out.astype(BF16)
    wg_bf, wu_bf, wd_bf = w_gate.astype(BF16), w_up.astype(BF16), w_down.astype(BF16)
    wr_pad = jnp.zeros((D_MODEL, LANES), BF16).at[:, :N_EXPERTS].set(w_router.astype(BF16))
    rb_pad = jnp.zeros((1, LANES), F32).at[0, :N_EXPERTS].set(router_bias)
    rope_tabs = _rope_tables(dec_seq, D_A) + _rope_tables(dec_seq, D_B)
    row = lambda a, l: a[l].reshape(1, -1)
    cak, cav, cbk, cbv = (a.reshape(*a.shape[:3], -1) for a in (cache_a_k, cache_a_v, cache_b_k, cache_b_v))

    def layer_step(x, l, state_bufs, latent):
        tok = dec_seq if latent else seq
        nb = dec_batch if latent else batch
        outs = _qkv(x, mod[l], w_in_bf, row(qn_g, l), row(kn_g, l), l, tok_per_batch=tok,
                    rope_tabs=rope_tabs if latent else None, state_bufs=state_bufs)
        qa, ka, va, qb, kb, vb = outs[:6]
        tq = min(TQ_ATTN, tok)
        oa = _attention(qa, ka, va, diff=True, n_batch=nb, tok_per_batch=tok, tq=tq, layer=l,
                        lam_init=_lambda_init(l), subln_g=row(subln_g, l),
                        lam_params=tuple(row(a, l) for a in (lam_q1, lam_k1, lam_q2, lam_k2)),
                        cache_k=cak if latent else None, cache_v=cav if latent else None)
        ob = _attention(qb, kb, vb, diff=False, n_batch=nb, tok_per_batch=tok, tq=tq, layer=l,
                        cache_k=cbk if latent else None, cache_v=cbv if latent else None)
        x1, h2, comb = _post(x, mod[l], oa, ob, w_in_bf, wa_bf, wb_bf, wo_bf, row(ln1_g, l), row(ln1_b, l),
                             wr_pad, rb_pad, l, tok_per_batch=tok, latent=latent)
        x2 = _moe(x1, h2, comb, mod[l], wg_bf, wu_bf, wd_bf, row(ln2_g, l), row(ln2_b, l), l,
                  tok_per_batch=tok, latent=latent)
        return x2, (outs[6:] if state_bufs is not None else None)

    wa_state = H_A * HEAD_W
    wb_state = G_B * HEAD_W
    state = tuple(jnp.zeros((batch, DEPTH, seq, w), F32) for w in (wa_state, wa_state, wb_state, wb_state))
    y = x_prompt.reshape(batch * seq, D_MODEL)
    for l in range(DEPTH):
        y, state = layer_step(y, l, state, latent=False)
    y_prompt = y.reshape(batch, seq, D_MODEL)

    y = x_sample.reshape(dec_batch * dec_seq, D_MODEL)
    for l in range(DEPTH):
        y, _ = layer_step(y, l, None, latent=True)
    y_sample = y.reshape(dec_batch, dec_seq, D_MODEL)

    sak, sav, sbk, sbv = state
    return (y_prompt, y_sample,
            sak.reshape(batch, DEPTH, seq, H_A, 2 * D_A), sav.reshape(batch, DEPTH, seq, H_A, 2 * D_A),
            sbk.reshape(batch, DEPTH, seq, G_B, D_B), sbv.reshape(batch, DEPTH, seq, G_B, D_B))
```

```python
import functools
import math

import jax
import jax.numpy as jnp
from jax import lax
from jax.experimental import pallas as pl
from jax.experimental.pallas import tpu as pltpu

D_MODEL = 1024
DEPTH = 4
GRID_W = 64
ROPE_THETA = 10000.0
H_A = 8
D_A = 64
H_B = 8
G_B = 4
D_B = 128
HEAD_W = 128
N_EXPERTS = 16
EXPERTS_PER_GROUP = 4
D_EXPERT = 512
ALPHA = (2 * DEPTH) ** 0.25
LN_EPS = 1e-6
RMS_EPS = 1e-6
W_QKV = 3 * H_A * 2 * D_A + H_B * D_B + 2 * G_B * D_B
LOG2E = 1.4426950408889634
LANES = 128
VMEM_LIMIT = 56 * 1024 * 1024

BF16 = jnp.bfloat16
F32 = jnp.float32

TM_QKV = 256
TM_POST = 512
TM_MOE = 1024
TN_MOD = 1536
TOP_K = 2
EXPERTS_PER_STEP = 2
SORT_ALIGN = 16
MOE_CHUNK = 160
PERM_CHUNK = 256
UNPERM_CHUNK = 768
TQ_ATTN = 1024
KEY_CHUNK = 512


def _params(n_axes, vmem=VMEM_LIMIT):
    return pltpu.CompilerParams(dimension_semantics=("arbitrary",) * n_axes,
                                vmem_limit_bytes=vmem)


def _mod_kernel(c_ref, w_ref, b_ref, o_ref):
    c = c_ref[...]
    s = (c * jax.nn.sigmoid(c)).astype(BF16)
    o_ref[...] = jnp.dot(s, w_ref[...].astype(BF16), preferred_element_type=F32) + b_ref[...]


def _modulation(cond8, w_mod, b_mod):
    n_out = 6 * D_MODEL
    return pl.pallas_call(
        _mod_kernel,
        grid=(DEPTH, n_out // TN_MOD),
        in_specs=[
            pl.BlockSpec((8, D_MODEL), lambda l, j: (0, 0)),
            pl.BlockSpec((None, D_MODEL, TN_MOD), lambda l, j: (l, 0, j)),
            pl.BlockSpec((None, 1, TN_MOD), lambda l, j: (l, 0, j)),
        ],
        out_specs=pl.BlockSpec((None, 8, TN_MOD), lambda l, j: (l, 0, j)),
        out_shape=jax.ShapeDtypeStruct((DEPTH, 8, n_out), F32),
        compiler_params=_params(2),
        name="modulation",
    )(cond8, w_mod, b_mod.reshape(DEPTH, 1, n_out))


def _rope_tables(n_tok, d):
    nf = d // 4
    n_rows = n_tok // GRID_W
    row = jnp.repeat(jnp.arange(n_rows, dtype=F32), GRID_W)
    col = jnp.tile(jnp.arange(GRID_W, dtype=F32), n_rows)
    inv = ROPE_THETA ** (-jnp.arange(nf, dtype=F32) / nf)
    lane = jnp.arange(LANES) % d
    axis = lane // (2 * nf)
    half = (lane % (2 * nf)) // nf
    freq = lane % nf
    pos = jnp.where(axis[None, :] == 0, row[:, None], col[:, None])
    ang = pos * inv[freq][None, :]
    cos, sin = jnp.cos(ang), jnp.sin(ang)
    sin_lo = jnp.where(half[None, :] == 0, -sin, 0.0)
    sin_hi = jnp.where(half[None, :] == 1, sin, 0.0)
    return cos, sin_lo, sin_hi


def _rope(x, cos, sin_lo, sin_hi, nf):
    up = pltpu.roll(x, LANES - nf, 1)
    down = pltpu.roll(x, nf, 1)
    return x * cos + up * sin_lo + down * sin_hi


def _rms(x, g):
    ms = jnp.mean(x * x, axis=-1, keepdims=True)
    return x * lax.rsqrt(ms + RMS_EPS) * g


def _qkv_kernel(*refs, rope, states):
    x_ref, mod_ref, w_ref, qn_ref, kn_ref = refs[:5]
    pos = 5
    if rope:
        ca, sa_lo, sa_hi, cb, sb_lo, sb_hi = (r[...] for r in refs[pos:pos + 6])
        pos += 6
    if states:
        pos += 4
    qa_ref, ka_ref, va_ref, qb_ref, kb_ref, vb_ref = refs[pos:pos + 6]
    pos += 6
    if states:
        sak_ref, sav_ref, sbk_ref, sbv_ref = refs[pos:pos + 4]

    sh1 = mod_ref[:, 0:D_MODEL]
    sc1 = mod_ref[:, D_MODEL:2 * D_MODEL]
    h = (x_ref[...] * (1.0 + sc1) + sh1).astype(BF16)

    def seg(lo, width):
        return jnp.dot(h, w_ref[:, lo:lo + width], preferred_element_type=F32)

    def heads(p, n):
        return [p[:, i * HEAD_W:(i + 1) * HEAD_W] for i in range(n)]

    qa_scale = (D_A ** -0.5) * LOG2E
    qb_scale = (D_B ** -0.5) * LOG2E
    nfa, nfb = D_A // 4, D_B // 4

    p = seg(0, H_A * HEAD_W)
    for i, c in enumerate(heads(p, H_A)):
        if rope:
            c = _rope(c, ca, sa_lo, sa_hi, nfa)
        qa_ref[i] = (c * qa_scale).astype(BF16)

    p = seg(H_A * HEAD_W, H_A * HEAD_W)
    if states:
        sak_ref[...] = p
    for i, c in enumerate(heads(p, H_A)):
        if rope:
            c = _rope(c, ca, sa_lo, sa_hi, nfa)
        ka_ref[i] = c.astype(BF16)

    p = seg(2 * H_A * HEAD_W, H_A * HEAD_W)
    if states:
        sav_ref[...] = p
    for i, c in enumerate(heads(p, H_A)):
        va_ref[i] = c.astype(BF16)

    off = 3 * H_A * HEAD_W
    p = seg(off, H_B * HEAD_W)
    qn = qn_ref[...]
    for i, c in enumerate(heads(p, H_B)):
        c = _rms(c, qn)
        if rope:
            c = _rope(c, cb, sb_lo, sb_hi, nfb)
        qb_ref[i] = (c * qb_scale).astype(BF16)

    off += H_B * HEAD_W
    p = seg(off, G_B * HEAD_W)
    kn = kn_ref[...]
    for i, c in enumerate(heads(p, G_B)):
        c = _rms(c, kn)
        if states:
            sbk_ref[:, i * HEAD_W:(i + 1) * HEAD_W] = c
        if rope:
            c = _rope(c, cb, sb_lo, sb_hi, nfb)
        kb_ref[i] = c.astype(BF16)

    off += G_B * HEAD_W
    p = seg(off, G_B * HEAD_W)
    if states:
        sbv_ref[...] = p
    for i, c in enumerate(heads(p, G_B)):
        vb_ref[i] = c.astype(BF16)


def _qkv(x, mod_rows, w_in_bf, qn_g, kn_g, layer, *, tok_per_batch, rope_tabs=None, state_bufs=None):
    n = x.shape[0]
    tm = min(TM_QKV, tok_per_batch)
    tiles_per_batch = tok_per_batch // tm
    rope = rope_tabs is not None
    states = state_bufs is not None
    row0 = 1 if rope else 0

    in_specs = [
        pl.BlockSpec((tm, D_MODEL), lambda i: (i, 0)),
        pl.BlockSpec((None, 1, 6 * D_MODEL), lambda i: (row0 + i // tiles_per_batch if rope else 0, 0, 0)),
        pl.BlockSpec((None, D_MODEL, W_QKV), lambda i: (layer, 0, 0), pipeline_mode=pl.Buffered(1)),
        pl.BlockSpec((1, HEAD_W), lambda i: (0, 0)),
        pl.BlockSpec((1, HEAD_W), lambda i: (0, 0)),
    ]
    args = [x, mod_rows, w_in_bf, qn_g, kn_g]
    if rope:
        in_specs += [pl.BlockSpec((tm, LANES), lambda i: (i % tiles_per_batch, 0))] * 6
        args += list(rope_tabs)
    aliases = {}
    if states:
        for k, buf in enumerate(state_bufs):
            aliases[len(args)] = 6 + k
            in_specs.append(pl.BlockSpec(memory_space=pl.ANY))
            args.append(buf)

    def hm(nh):
        return (pl.BlockSpec((nh, tm, HEAD_W), lambda i: (0, i, 0)),
                jax.ShapeDtypeStruct((nh, n, HEAD_W), BF16))

    outs = [hm(H_A), hm(H_A), hm(H_A), hm(H_B), hm(G_B), hm(G_B)]
    if states:
        assert tm == tok_per_batch
        for buf in state_bufs:
            outs.append((pl.BlockSpec((None, None, tm, buf.shape[-1]), lambda i: (i, layer, 0, 0)),
                         jax.ShapeDtypeStruct(buf.shape, F32)))
    return pl.pallas_call(
        functools.partial(_qkv_kernel, rope=rope, states=states),
        grid=(n // tm,),
        in_specs=in_specs,
        out_specs=[o[0] for o in outs],
        out_shape=[o[1] for o in outs],
        input_output_aliases=aliases,
        compiler_params=_params(1),
        name="qkv_states" if states else "qkv_rope",
    )(*args)


def _attn_kernel(*refs, diff, n_cache, n_new, tq, kc, hb, lam_init):
    if diff:
        lq1, lk1, lq2, lk2, g_ref = refs[:5]
        refs = refs[5:]
    q_ref, kn_ref, vn_ref = refs[:3]
    refs = refs[3:]
    if n_cache:
        kc_ref, vc_ref = refs[:2]
        refs = refs[2:]
    o_ref = refs[0]
    n_keys = n_cache + n_new

    def transposed(x):
        return x.astype(F32).T.astype(BF16)

    if n_cache:
        assert hb == 1
        k_all, vt_all = refs[1:3]

        @pl.when(pl.program_id(2) == 0)
        def _():
            k_all[n_cache:n_keys, :] = kn_ref[0]
            vt_all[:, n_cache:n_keys] = transposed(vn_ref[0])

        for head in range(kc_ref.shape[1]):
            @pl.when((pl.program_id(2) == 0) & (pl.program_id(1) == head))
            def _(head=head):
                k_all[0:n_cache, :] = kc_ref[:, head, :].astype(BF16)
                vt_all[:, 0:n_cache] = transposed(vc_ref[:, head, :])

        k_chunk = lambda j, c: k_all[c * kc:(c + 1) * kc, :]
        vt_chunk = lambda j, c: vt_all[:, c * kc:(c + 1) * kc]
    else:
        vts = [transposed(vn_ref[j]) for j in range(hb)]
        k_chunk = lambda j, c: kn_ref[j, c * kc:(c + 1) * kc, :]
        vt_chunk = lambda j, c: vts[j][:, c * kc:(c + 1) * kc]

    def q_columns(j):
        if diff:
            qt = q_ref[j].astype(F32).T
            row = lax.broadcasted_iota(jnp.int32, qt.shape, 0)
            qt = jnp.concatenate([jnp.where(row < D_A, qt, 0.0), jnp.where(row >= D_A, qt, 0.0)], axis=1)
        else:
            qt = jnp.concatenate([q_ref[2 * j].astype(F32).T, q_ref[2 * j + 1].astype(F32).T], axis=1)
        return qt.astype(BF16)

    qts = [q_columns(j) for j in range(hb)]

    def scores(item):
        j, c = item
        return jnp.dot(k_chunk(j, c), qts[j], preferred_element_type=F32)

    if diff:
        lam = (jnp.exp(jnp.sum(lq1[...] * lk1[...], axis=-1, keepdims=True))
               - jnp.exp(jnp.sum(lq2[...] * lk2[...], axis=-1, keepdims=True)) + lam_init)

    def finish(j, l, acc):
        inv = 1.0 / l
        if diff:
            o = acc[:, :tq] * inv[:, :tq] - acc[:, tq:] * (lam * inv[:, tq:])
            o = _rms(o.T, g_ref[...]) * (1.0 - lam_init)
            o_ref[j] = o.astype(BF16)
        else:
            o = acc * inv
            o_ref[2 * j] = o[:, :tq].T.astype(BF16)
            o_ref[2 * j + 1] = o[:, tq:].T.astype(BF16)

    n_chunks = n_keys // kc
    items = [(j, c) for j in range(hb) for c in range(n_chunks)]
    m = l = acc = None
    s_next = scores(items[0])
    for idx, (j, c) in enumerate(items):
        s = s_next
        if idx + 1 < len(items):
            s_next = scores(items[idx + 1])
        m_c = jnp.max(s, axis=0, keepdims=True)
        m_new = m_c if c == 0 else jnp.maximum(m, m_c)
        e = jnp.exp2(s - m_new)
        l_c = jnp.sum(e, axis=0, keepdims=True)
        pv = jnp.dot(vt_chunk(j, c), e.astype(BF16), preferred_element_type=F32)
        if c == 0:
            l, acc = l_c, pv
        else:
            alpha = jnp.exp2(m - m_new)
            l = alpha * l + l_c
            acc = alpha * acc + pv
        m = m_new
        if c == n_chunks - 1:
            finish(j, l, acc)


def _attention(q, k_new, v_new, *, diff, n_batch, tok_per_batch, tq, layer, lam_init=0.0,
               lam_params=None, subln_g=None, cache_k=None, cache_v=None):
    n_kv = k_new.shape[0]
    n = q.shape[1]
    nq = tok_per_batch // tq
    n_cache = 0 if cache_k is None else cache_k.shape[2]
    n_keys = n_cache + tok_per_batch
    kc = math.gcd(n_keys, KEY_CHUNK)
    hb = 1 if n_cache else n_kv

    in_specs, args = [], []
    if diff:
        in_specs += [pl.BlockSpec((1, D_A), lambda b, h, i: (0, 0))] * 4
        args += list(lam_params)
        in_specs.append(pl.BlockSpec((1, HEAD_W), lambda b, h, i: (0, 0)))
        args.append(subln_g)
    q_block = (hb if diff else 2 * hb, tq, HEAD_W)
    in_specs += [
        pl.BlockSpec(q_block, lambda b, h, i: (h, b * nq + i, 0)),
        pl.BlockSpec((hb, tok_per_batch, HEAD_W), lambda b, h, i: (h, b, 0)),
        pl.BlockSpec((hb, tok_per_batch, HEAD_W), lambda b, h, i: (h, b, 0)),
    ]
    args += [q, k_new, v_new]
    scratch = []
    if n_cache:
        cspec = pl.BlockSpec((None, None, n_cache, n_kv, HEAD_W), lambda b, h, i: (b, layer, 0, 0, 0))
        in_specs += [cspec, cspec]
        args += [cache_k, cache_v]
        scratch = [pltpu.VMEM((n_keys, HEAD_W), BF16), pltpu.VMEM((HEAD_W, n_keys), BF16)]
    return pl.pallas_call(
        functools.partial(_attn_kernel, diff=diff, n_cache=n_cache, n_new=tok_per_batch, tq=tq, kc=kc,
                          hb=hb, lam_init=lam_init),
        grid=(n_batch, n_kv // hb, nq),
        in_specs=in_specs,
        out_specs=pl.BlockSpec(q_block, lambda b, h, i: (h, b * nq + i, 0)),
        out_shape=jax.ShapeDtypeStruct((H_A, n, HEAD_W), BF16),
        scratch_shapes=scratch,
        compiler_params=_params(3),
        name=("attn_diff" if diff else "attn_gqa") + ("_cache" if n_cache else ""),
    )(*args)


def _layer_norm(y, g, b):
    mu = jnp.mean(y, axis=-1, keepdims=True)
    yc = y - mu
    var = jnp.mean(yc * yc, axis=-1, keepdims=True)
    return yc * lax.rsqrt(var + LN_EPS) * g + b


def _route(h2, wr_ref, rb_ref):
    logits = jnp.dot(h2, wr_ref[...], preferred_element_type=F32)
    scores = jax.nn.sigmoid(logits)
    biased = scores + rb_ref[...]
    lane = lax.broadcasted_iota(jnp.int32, biased.shape, 1)
    valid = lane < N_EXPERTS
    pos = lane & (EXPERTS_PER_GROUP - 1)

    def neighbour(kk):
        fwd = pltpu.roll(biased, LANES - kk, 1)
        bwd = pltpu.roll(biased, EXPERTS_PER_GROUP - kk, 1)
        wrapped = pos + kk >= EXPERTS_PER_GROUP
        return jnp.where(wrapped, bwd, fwd), wrapped

    (v1, w1), (v2, w2), (v3, w3) = neighbour(1), neighbour(2), neighbour(3)
    rank = jnp.zeros(biased.shape, jnp.int32)
    for v, wrapped in ((v1, w1), (v2, w2), (v3, w3)):
        ge = jnp.where(v >= biased, 1, 0)
        gt = jnp.where(v > biased, 1, 0)
        rank = rank + jnp.where(wrapped, ge, gt)
    hi_a, lo_a = jnp.maximum(biased, v1), jnp.minimum(biased, v1)
    hi_b, lo_b = jnp.maximum(v2, v3), jnp.minimum(v2, v3)
    top1 = jnp.maximum(hi_a, hi_b)
    top2 = jnp.maximum(jnp.minimum(hi_a, hi_b), jnp.maximum(lo_a, lo_b))
    grp_score = jnp.where(valid, top1 + top2, -jnp.inf)
    best = jnp.max(grp_score, axis=-1, keepdims=True)
    grp = (lane // EXPERTS_PER_GROUP).astype(F32)
    first = jnp.min(jnp.where(grp_score == best, grp, float(LANES)), axis=-1, keepdims=True)
    chosen = jnp.where(grp == first, rank, EXPERTS_PER_GROUP) < 2
    w_raw = jnp.where(chosen, scores, 0.0)
    return w_raw / jnp.sum(w_raw, axis=-1, keepdims=True)


def _post_kernel(x_ref, mod_ref, oa_ref, ob_ref, wga_ref, wgb_ref, wa_ref, wb_ref, wo_ref, lg_ref, lb_ref,
                 wr_ref, rb_ref, x1_ref, h2_ref, comb_ref):
    x = x_ref[...]
    sh1 = mod_ref[:, 0:D_MODEL]
    sc1 = mod_ref[:, D_MODEL:2 * D_MODEL]
    g1 = mod_ref[:, 2 * D_MODEL:3 * D_MODEL]
    sh2 = mod_ref[:, 3 * D_MODEL:4 * D_MODEL]
    sc2 = mod_ref[:, 4 * D_MODEL:5 * D_MODEL]
    h = (x * (1.0 + sc1) + sh1).astype(BF16)
    oa = jnp.concatenate([oa_ref[i] for i in range(H_A)], axis=1)
    ob = jnp.concatenate([ob_ref[i] for i in range(H_B)], axis=1)
    ga = jax.nn.sigmoid(jnp.dot(h, wga_ref[...], preferred_element_type=F32))
    merged = ga * jnp.dot(oa, wa_ref[...], preferred_element_type=F32)
    gb = jax.nn.sigmoid(jnp.dot(h, wgb_ref[...], preferred_element_type=F32))
    merged = merged + gb * jnp.dot(ob, wb_ref[...], preferred_element_type=F32)
    mix = jnp.dot(merged.astype(BF16), wo_ref[...], preferred_element_type=F32)
    x1 = _layer_norm(ALPHA * x + g1 * mix, lg_ref[...], lb_ref[...])
    x1_ref[...] = x1
    h2 = (x1 * (1.0 + sc2) + sh2).astype(BF16)
    h2_ref[...] = h2
    comb_ref[...] = _route(h2, wr_ref, rb_ref)


def _post(x, mod_rows, oa, ob, w_in_bf, wa_bf, wb_bf, wo_bf, ln_g, ln_b, wr_pad, rb_pad, layer, *,
          tok_per_batch, latent):
    n = x.shape[0]
    tm = min(TM_POST, tok_per_batch if latent else n)
    tiles_per_batch = max(tok_per_batch // tm, 1)
    const2 = lambda i: (0, 0)
    once = dict(pipeline_mode=pl.Buffered(1))
    wspec = pl.BlockSpec((None, D_MODEL, D_MODEL), lambda i: (layer, 0, 0), **once)
    return pl.pallas_call(
        _post_kernel,
        grid=(n // tm,),
        in_specs=[
            pl.BlockSpec((tm, D_MODEL), lambda i: (i, 0)),
            pl.BlockSpec((None, 1, 6 * D_MODEL), lambda i: (1 + i // tiles_per_batch if latent else 0, 0, 0)),
            pl.BlockSpec((H_A, tm, HEAD_W), lambda i: (0, i, 0)),
            pl.BlockSpec((H_B, tm, HEAD_W), lambda i: (0, i, 0)),
            pl.BlockSpec((None, D_MODEL, D_MODEL), lambda i: (layer, 0, W_QKV // D_MODEL), **once),
            pl.BlockSpec((None, D_MODEL, D_MODEL), lambda i: (layer, 0, W_QKV // D_MODEL + 1), **once),
            wspec, wspec, wspec,
            pl.BlockSpec((1, D_MODEL), const2),
            pl.BlockSpec((1, D_MODEL), const2),
            pl.BlockSpec((D_MODEL, LANES), const2),
            pl.BlockSpec((1, LANES), const2),
        ],
        out_specs=[
            pl.BlockSpec((tm, D_MODEL), lambda i: (i, 0)),
            pl.BlockSpec((tm, D_MODEL), lambda i: (i, 0)),
            pl.BlockSpec((tm, LANES), lambda i: (i, 0)),
        ],
        out_shape=[
            jax.ShapeDtypeStruct((n, D_MODEL), F32),
            jax.ShapeDtypeStruct((n, D_MODEL), BF16),
            jax.ShapeDtypeStruct((n, LANES), F32),
        ],
        compiler_params=_params(1),
        name="post_attn",
    )(x, mod_rows, oa, ob, w_in_bf, w_in_bf, wa_bf, wb_bf, wo_bf, ln_g, ln_b, wr_pad, rb_pad)


def _one_hot(idx, a, b):
    return jnp.where(idx == a, 1.0, jnp.where(idx == b, 1.0, 0.0)).astype(BF16)


def _moe_kernel(x1_ref, h2_ref, comb_ref, mod_ref, tri_ref, wg_ref, wu_ref, wd_ref, lg_ref, lb_ref, o_ref,
                hs_ref, cs_ref, ys_ref, pos_ref, meta_ref, *, tm, n_rows, unperm):
    i, step = pl.program_id(0), pl.program_id(1)
    n_alloc = hs_ref.shape[0]

    @pl.when(step == 0)
    def _():
        @pl.when(i == 0)
        def _():
            hs_ref[n_rows:n_alloc, :] = jnp.zeros((n_alloc - n_rows, D_MODEL), BF16)
            cs_ref[n_rows:n_alloc, :] = jnp.zeros((n_alloc - n_rows, LANES), F32)
            ys_ref[...] = jnp.zeros_like(ys_ref)

        comb = comb_ref[...]
        sel = comb > 0.0
        lane = lax.broadcasted_iota(jnp.int32, comb.shape, 1)
        ones = jnp.where(sel, 1.0, 0.0).astype(BF16)
        cum = jnp.dot(tri_ref[...], ones, preferred_element_type=F32)
        cnt = jnp.broadcast_to(cum[tm - 1:tm, :], (8, LANES)).astype(jnp.int32)
        padded = (cnt + (SORT_ALIGN - 1)) & -SORT_ALIGN
        lane8 = lax.broadcasted_iota(jnp.int32, padded.shape, 1)
        incl = padded
        for d in (1, 2, 4, 8):
            incl = incl + jnp.where(lane8 >= d, pltpu.roll(incl, d, 1), 0)
        off = incl - padded
        for k in range(N_EXPERTS):
            meta_ref[0, k] = off[0, k]
            meta_ref[1, k] = cnt[0, k]

        pos = off[0:1, :].astype(F32) + cum - 1.0
        p_lo = jnp.min(jnp.where(sel, pos, float(n_alloc)), axis=1, keepdims=True)
        p_hi = jnp.max(jnp.where(sel, pos, -1.0), axis=1, keepdims=True)
        pos_cols = jnp.where(lane == 0, p_lo, jnp.where(lane == 1, p_hi, 0.0))
        pos_ref[...] = pos_cols
        pos_rows = pos_cols.T
        r_lo, r_hi = pos_rows[0:1, :], pos_rows[1:2, :]

        c0 = comb.astype(BF16).astype(F32)
        c1 = (comb - c0).astype(BF16).astype(F32)
        c2 = (comb - c0 - c1).astype(BF16).astype(F32)
        pieces = (c0 + pltpu.roll(c1, N_EXPERTS, 1) + pltpu.roll(c2, 2 * N_EXPERTS, 1)).astype(BF16)
        h2 = h2_ref[...]
        for rc in range(n_rows // PERM_CHUNK):
            lo = rc * PERM_CHUNK
            ridx = (lax.broadcasted_iota(jnp.int32, (PERM_CHUNK, tm), 0) + lo).astype(F32)
            p = _one_hot(ridx, r_lo, r_hi)
            hs_ref[lo:lo + PERM_CHUNK, :] = jnp.dot(p, h2, preferred_element_type=F32).astype(BF16)
            c3 = jnp.dot(p, pieces, preferred_element_type=F32)
            cs_ref[lo:lo + PERM_CHUNK, :] = (c3 + pltpu.roll(c3, LANES - N_EXPERTS, 1)
                                             + pltpu.roll(c3, LANES - 2 * N_EXPERTS, 1))

    lane_c = lax.broadcasted_iota(jnp.int32, (MOE_CHUNK, LANES), 1)
    for k in range(EXPERTS_PER_STEP):
        e = step * EXPERTS_PER_STEP + k
        off_e, cnt_e = meta_ref[0, e], meta_ref[1, e]

        def chunk(j, carry, k=k, e=e, off_e=off_e):
            start = pl.multiple_of(off_e + j * MOE_CHUNK, SORT_ALIGN)
            rows = hs_ref[pl.ds(start, MOE_CHUNK), :]
            w = jnp.sum(jnp.where(lane_c == e, cs_ref[pl.ds(start, MOE_CHUNK), :], 0.0), axis=1, keepdims=True)
            g = jnp.dot(rows, wg_ref[k], preferred_element_type=F32)
            u = jnp.dot(rows, wu_ref[k], preferred_element_type=F32)
            act = (g * jax.nn.sigmoid(g)) * u * w
            y = jnp.dot(act.astype(BF16), wd_ref[k], preferred_element_type=F32)
            ys_ref[pl.ds(start, MOE_CHUNK), :] = y.astype(BF16)
            return carry

        lax.fori_loop(0, (cnt_e + (MOE_CHUNK - 1)) // MOE_CHUNK, chunk, 0)

    @pl.when(step == N_EXPERTS // EXPERTS_PER_STEP - 1)
    def _():
        p_lo, p_hi = pos_ref[:, 0:1], pos_ref[:, 1:2]
        y = None
        for cc in range(n_rows // unperm):
            lo = cc * unperm
            cidx = (lax.broadcasted_iota(jnp.int32, (tm, unperm), 1) + lo).astype(F32)
            q = _one_hot(cidx, p_lo, p_hi)
            part = jnp.dot(q, ys_ref[lo:lo + unperm, :], preferred_element_type=F32)
            y = part if y is None else y + part
        g2 = mod_ref[:, 5 * D_MODEL:6 * D_MODEL]
        o_ref[...] = _layer_norm(ALPHA * x1_ref[...] + g2 * y, lg_ref[...], lb_ref[...])


def _moe(x1, h2, comb, mod_rows, w_gate, w_up, w_down, ln_g, ln_b, layer, *, tok_per_batch, latent):
    n = x1.shape[0]
    tm = min(TM_MOE, tok_per_batch if latent else n)
    tiles_per_batch = max(tok_per_batch // tm, 1)
    n_rows = TOP_K * tm + N_EXPERTS * SORT_ALIGN
    assert n_rows % PERM_CHUNK == 0
    unperm = math.gcd(n_rows, UNPERM_CHUNK)
    n_alloc = n_rows + MOE_CHUNK
    tri = jnp.tril(jnp.ones((tm, tm), BF16))
    const2 = lambda i, e: (0, 0)
    once = dict(pipeline_mode=pl.Buffered(1))
    return pl.pallas_call(
        functools.partial(_moe_kernel, tm=tm, n_rows=n_rows, unperm=unperm),
        grid=(n // tm, N_EXPERTS // EXPERTS_PER_STEP),
        in_specs=[
            pl.BlockSpec((tm, D_MODEL), lambda i, e: (i, 0), **once),
            pl.BlockSpec((tm, D_MODEL), lambda i, e: (i, 0), **once),
            pl.BlockSpec((tm, LANES), lambda i, e: (i, 0), **once),
            pl.BlockSpec((None, 1, 6 * D_MODEL),
                         lambda i, e: (1 + i // tiles_per_batch if latent else 0, 0, 0)),
            pl.BlockSpec((tm, tm), const2, **once),
            pl.BlockSpec((None, EXPERTS_PER_STEP, D_MODEL, D_EXPERT), lambda i, e: (layer, e, 0, 0)),
            pl.BlockSpec((None, EXPERTS_PER_STEP, D_MODEL, D_EXPERT), lambda i, e: (layer, e, 0, 0)),
            pl.BlockSpec((None, EXPERTS_PER_STEP, D_EXPERT, D_MODEL), lambda i, e: (layer, e, 0, 0)),
            pl.BlockSpec((1, D_MODEL), const2),
            pl.BlockSpec((1, D_MODEL), const2),
        ],
        out_specs=pl.BlockSpec((tm, D_MODEL), lambda i, e: (i, 0)),
        out_shape=jax.ShapeDtypeStruct((n, D_MODEL), F32),
        scratch_shapes=[
            pltpu.VMEM((n_alloc, D_MODEL), BF16),
            pltpu.VMEM((n_alloc, LANES), F32),
            pltpu.VMEM((n_alloc, D_MODEL), BF16),
            pltpu.VMEM((tm, LANES), F32),
            pltpu.SMEM((2, N_EXPERTS), jnp.int32),
        ],
        compiler_params=_params(2),
        name="moe_ln2",
    )(x1, h2, comb, mod_rows, tri, w_gate, w_up, w_down, ln_g, ln_b)


def _lambda_init(layer):
    return 0.8 - 0.6 * math.exp(-0.3 * layer)


def kernel(x_prompt, x_sample, cache_a_k, cache_a_v, cache_b_k, cache_b_v, c, c_ctx, w_mod, b_mod, w_in,
           lam_q1, lam_k1, lam_q2, lam_k2, subln_g, qn_g, kn_g, w_br_a, w_br_b, w_out, ln1_g, ln1_b,
           ln2_g, ln2_b, w_router, router_bias, w_gate, w_up, w_down):
    batch, seq, _ = x_prompt.shape
    dec_batch, dec_seq, _ = x_sample.shape

    cond8 = jnp.zeros((8, D_MODEL), F32).at[0].set(c_ctx).at[1:1 + dec_batch].set(c)
    mod = _modulation(cond8, w_mod, b_mod).reshape(DEPTH, 8, 1, 6 * D_MODEL)

    w_in_bf = w_in.astype(BF16)
    wa_bf, wb_bf, wo_bf = w_br_a.astype(BF16), w_br_b.astype(BF16), w_out.astype(BF16)
    wg_bf, wu_bf, wd_bf = w_gate.astype(BF16), w_up.astype(BF16), w_down.astype(BF16)
    wr_pad = jnp.zeros((D_MODEL, LANES), BF16).at[:, :N_EXPERTS].set(w_router.astype(BF16))
    rb_pad = jnp.zeros((1, LANES), F32).at[0, :N_EXPERTS].set(router_bias)
    rope_tabs = _rope_tables(dec_seq, D_A) + _rope_tables(dec_seq, D_B)
    row = lambda a, l: a[l].reshape(1, -1)
    cak, cav, cbk, cbv = cache_a_k, cache_a_v, cache_b_k, cache_b_v

    def layer_step(x, l, state_bufs, latent):
        tok = dec_seq if latent else seq
        nb = dec_batch if latent else batch
        outs = _qkv(x, mod[l], w_in_bf, row(qn_g, l), row(kn_g, l), l, tok_per_batch=tok,
                    rope_tabs=rope_tabs if latent else None, state_bufs=state_bufs)
        qa, ka, va, qb, kb, vb = outs[:6]
        tq = min(TQ_ATTN, tok)
        oa = _attention(qa, ka, va, diff=True, n_batch=nb, tok_per_batch=tok, tq=tq, layer=l,
                        lam_init=_lambda_init(l), subln_g=row(subln_g, l),
                        lam_params=tuple(row(a, l) for a in (lam_q1, lam_k1, lam_q2, lam_k2)),
                        cache_k=cak if latent else None, cache_v=cav if latent else None)
        ob = _attention(qb, kb, vb, diff=False, n_batch=nb, tok_per_batch=tok, tq=tq, layer=l,
                        cache_k=cbk if latent else None, cache_v=cbv if latent else None)
        x1, h2, comb = _post(x, mod[l], oa, ob, w_in_bf, wa_bf, wb_bf, wo_bf, row(ln1_g, l), row(ln1_b, l),
                             wr_pad, rb_pad, l, tok_per_batch=tok, latent=latent)
        x2 = _moe(x1, h2, comb, mod[l], wg_bf, wu_bf, wd_bf, row(ln2_g, l), row(ln2_b, l), l,
                  tok_per_batch=tok, latent=latent)
        return x2, (outs[6:] if state_bufs is not None else None)

    wa_state = H_A * HEAD_W
    wb_state = G_B * HEAD_W
    state = tuple(jnp.zeros((batch, DEPTH, seq, w), F32) for w in (wa_state, wa_state, wb_state, wb_state))
    y = x_prompt.reshape(batch * seq, D_MODEL)
    for l in range(DEPTH):
        y, state = layer_step(y, l, state, latent=False)
    y_prompt = y.reshape(batch, seq, D_MODEL)

    y = x_sample.reshape(dec_batch * dec_seq, D_MODEL)
    for l in range(DEPTH):
        y, _ = layer_step(y, l, None, latent=True)
    y_sample = y.reshape(dec_batch, dec_seq, D_MODEL)

    sak, sav, sbk, sbv = state
    return (y_prompt, y_sample,
            sak.reshape(batch, DEPTH, seq, H_A, 2 * D_A), sav.reshape(batch, DEPTH, seq, H_A, 2 * D_A),
            sbk.reshape(batch, DEPTH, seq, G_B, D_B), sbv.reshape(batch, DEPTH, seq, G_B, D_B))
```

```python
import functools
import math

import jax
import jax.numpy as jnp
from jax import lax
from jax.experimental import pallas as pl
from jax.experimental.pallas import tpu as pltpu

D_MODEL = 1024
DEPTH = 4
GRID_W = 64
ROPE_THETA = 10000.0
H_A = 8
D_A = 64
H_B = 8
G_B = 4
D_B = 128
HEAD_W = 128
N_EXPERTS = 16
EXPERTS_PER_GROUP = 4
D_EXPERT = 512
ALPHA = (2 * DEPTH) ** 0.25
LN_EPS = 1e-6
RMS_EPS = 1e-6
W_QKV = 3 * H_A * 2 * D_A + H_B * D_B + 2 * G_B * D_B
LOG2E = 1.4426950408889634
LANES = 128
VMEM_LIMIT = 56 * 1024 * 1024

BF16 = jnp.bfloat16
F32 = jnp.float32

TM_QKV = 256
TM_POST = 512
POST_ROWS = 256
TM_MOE = 1024
TN_MOD = 1536
TOP_K = 2
EXPERTS_PER_STEP = 2
SORT_ALIGN = 16
MOE_CHUNK = 160
PERM_CHUNK = 256
UNPERM_CHUNK = 768
TQ_ATTN = 1024
HEADS_PER_STEP_CACHED = 2
KEY_CHUNK = 512


def _params(n_axes, vmem=VMEM_LIMIT):
    return pltpu.CompilerParams(dimension_semantics=("arbitrary",) * n_axes,
                                vmem_limit_bytes=vmem)


def _mod_kernel(c_ref, w_ref, b_ref, o_ref):
    c = c_ref[...]
    s = (c * jax.nn.sigmoid(c)).astype(BF16)
    o_ref[...] = jnp.dot(s, w_ref[...].astype(BF16), preferred_element_type=F32) + b_ref[...]


def _modulation(cond8, w_mod, b_mod):
    n_out = 6 * D_MODEL
    return pl.pallas_call(
        _mod_kernel,
        grid=(DEPTH, n_out // TN_MOD),
        in_specs=[
            pl.BlockSpec((8, D_MODEL), lambda l, j: (0, 0)),
            pl.BlockSpec((None, D_MODEL, TN_MOD), lambda l, j: (l, 0, j)),
            pl.BlockSpec((None, 1, TN_MOD), lambda l, j: (l, 0, j)),
        ],
        out_specs=pl.BlockSpec((None, 8, TN_MOD), lambda l, j: (l, 0, j)),
        out_shape=jax.ShapeDtypeStruct((DEPTH, 8, n_out), F32),
        compiler_params=_params(2),
        name="modulation",
    )(cond8, w_mod, b_mod.reshape(DEPTH, 1, n_out))


def _rope_tables(n_tok, d):
    nf = d // 4
    n_rows = n_tok // GRID_W
    row = jnp.repeat(jnp.arange(n_rows, dtype=F32), GRID_W)
    col = jnp.tile(jnp.arange(GRID_W, dtype=F32), n_rows)
    inv = ROPE_THETA ** (-jnp.arange(nf, dtype=F32) / nf)
    lane = jnp.arange(LANES) % d
    axis = lane // (2 * nf)
    half = (lane % (2 * nf)) // nf
    freq = lane % nf
    pos = jnp.where(axis[None, :] == 0, row[:, None], col[:, None])
    ang = pos * inv[freq][None, :]
    cos, sin = jnp.cos(ang), jnp.sin(ang)
    sin_lo = jnp.where(half[None, :] == 0, -sin, 0.0)
    sin_hi = jnp.where(half[None, :] == 1, sin, 0.0)
    return cos, sin_lo, sin_hi


def _rope(x, cos, sin_lo, sin_hi, nf):
    up = pltpu.roll(x, LANES - nf, 1)
    down = pltpu.roll(x, nf, 1)
    return x * cos + up * sin_lo + down * sin_hi


def _rms(x, g):
    ms = jnp.mean(x * x, axis=-1, keepdims=True)
    return x * lax.rsqrt(ms + RMS_EPS) * g


def _qkv_kernel(*refs, rope, states):
    x_ref, mod_ref, w_ref, qn_ref, kn_ref = refs[:5]
    pos = 5
    if rope:
        ca, sa_lo, sa_hi, cb, sb_lo, sb_hi = (r[...] for r in refs[pos:pos + 6])
        pos += 6
    if states:
        pos += 4
    qa_ref, ka_ref, va_ref, qb_ref, kb_ref, vb_ref = refs[pos:pos + 6]
    pos += 6
    if states:
        sak_ref, sav_ref, sbk_ref, sbv_ref = refs[pos:pos + 4]

    sh1 = mod_ref[:, 0:D_MODEL]
    sc1 = mod_ref[:, D_MODEL:2 * D_MODEL]
    h = (x_ref[...] * (1.0 + sc1) + sh1).astype(BF16)

    def seg(lo, width):
        return jnp.dot(h, w_ref[:, lo:lo + width], preferred_element_type=F32)

    def heads(p, n):
        return [p[:, i * HEAD_W:(i + 1) * HEAD_W] for i in range(n)]

    qa_scale = (D_A ** -0.5) * LOG2E
    qb_scale = (D_B ** -0.5) * LOG2E
    nfa, nfb = D_A // 4, D_B // 4

    p = seg(0, H_A * HEAD_W)
    for i, c in enumerate(heads(p, H_A)):
        if rope:
            c = _rope(c, ca, sa_lo, sa_hi, nfa)
        qa_ref[i] = (c * qa_scale).astype(BF16)

    p = seg(H_A * HEAD_W, H_A * HEAD_W)
    if states:
        sak_ref[...] = p
    for i, c in enumerate(heads(p, H_A)):
        if rope:
            c = _rope(c, ca, sa_lo, sa_hi, nfa)
        ka_ref[i] = c.astype(BF16)

    p = seg(2 * H_A * HEAD_W, H_A * HEAD_W)
    if states:
        sav_ref[...] = p
    for i, c in enumerate(heads(p, H_A)):
        va_ref[i] = c.astype(BF16)

    off = 3 * H_A * HEAD_W
    p = seg(off, H_B * HEAD_W)
    qn = qn_ref[...]
    for i, c in enumerate(heads(p, H_B)):
        c = _rms(c, qn)
        if rope:
            c = _rope(c, cb, sb_lo, sb_hi, nfb)
        qb_ref[i] = (c * qb_scale).astype(BF16)

    off += H_B * HEAD_W
    p = seg(off, G_B * HEAD_W)
    kn = kn_ref[...]
    for i, c in enumerate(heads(p, G_B)):
        c = _rms(c, kn)
        if states:
            sbk_ref[:, i * HEAD_W:(i + 1) * HEAD_W] = c
        if rope:
            c = _rope(c, cb, sb_lo, sb_hi, nfb)
        kb_ref[i] = c.astype(BF16)

    off += G_B * HEAD_W
    p = seg(off, G_B * HEAD_W)
    if states:
        sbv_ref[...] = p
    for i, c in enumerate(heads(p, G_B)):
        vb_ref[i] = c.astype(BF16)


def _qkv(x, mod_rows, w_in_bf, qn_g, kn_g, layer, *, tok_per_batch, rope_tabs=None, state_bufs=None):
    n = x.shape[0]
    tm = min(TM_QKV, tok_per_batch)
    tiles_per_batch = tok_per_batch // tm
    rope = rope_tabs is not None
    states = state_bufs is not None
    row0 = 1 if rope else 0

    in_specs = [
        pl.BlockSpec((tm, D_MODEL), lambda i: (i, 0)),
        pl.BlockSpec((None, 1, 6 * D_MODEL), lambda i: (row0 + i // tiles_per_batch if rope else 0, 0, 0)),
        pl.BlockSpec((None, D_MODEL, W_QKV), lambda i: (layer, 0, 0), pipeline_mode=pl.Buffered(1)),
        pl.BlockSpec((1, HEAD_W), lambda i: (0, 0)),
        pl.BlockSpec((1, HEAD_W), lambda i: (0, 0)),
    ]
    args = [x, mod_rows, w_in_bf, qn_g, kn_g]
    if rope:
        in_specs += [pl.BlockSpec((tm, LANES), lambda i: (i % tiles_per_batch, 0))] * 6
        args += list(rope_tabs)
    aliases = {}
    if states:
        for k, buf in enumerate(state_bufs):
            aliases[len(args)] = 6 + k
            in_specs.append(pl.BlockSpec(memory_space=pl.ANY))
            args.append(buf)

    def hm(nh):
        return (pl.BlockSpec((nh, tm, HEAD_W), lambda i: (0, i, 0)),
                jax.ShapeDtypeStruct((nh, n, HEAD_W), BF16))

    outs = [hm(H_A), hm(H_A), hm(H_A), hm(H_B), hm(G_B), hm(G_B)]
    if states:
        assert tm == tok_per_batch
        for buf in state_bufs:
            outs.append((pl.BlockSpec((None, None, tm, buf.shape[-1]), lambda i: (i, layer, 0, 0)),
                         jax.ShapeDtypeStruct(buf.shape, F32)))
    return pl.pallas_call(
        functools.partial(_qkv_kernel, rope=rope, states=states),
        grid=(n // tm,),
        in_specs=in_specs,
        out_specs=[o[0] for o in outs],
        out_shape=[o[1] for o in outs],
        input_output_aliases=aliases,
        compiler_params=_params(1),
        name="qkv_states" if states else "qkv_rope",
    )(*args)


def _attn_kernel(*refs, diff, n_cache, n_new, tq, kc, hb, lam_init):
    if diff:
        lq1, lk1, lq2, lk2, g_ref = refs[:5]
        refs = refs[5:]
    q_ref, kn_ref, vn_ref = refs[:3]
    refs = refs[3:]
    if n_cache:
        kc_ref, vc_ref = refs[:2]
        refs = refs[2:]
    o_ref = refs[0]
    n_keys = n_cache + n_new

    def transposed(x):
        return x.astype(F32).T.astype(BF16)

    if n_cache:
        k_all, vt_all = refs[1:3]

        @pl.when(pl.program_id(2) == 0)
        def _():
            for j in range(hb):
                k_all[j, n_cache:n_keys, :] = kn_ref[j]
                vt_all[j, :, n_cache:n_keys] = transposed(vn_ref[j])

        for head in range(kc_ref.shape[1]):
            @pl.when((pl.program_id(2) == 0) & (pl.program_id(1) == head // hb))
            def _(head=head):
                k_all[head % hb, 0:n_cache, :] = kc_ref[:, head, :].astype(BF16)
                vt_all[head % hb, :, 0:n_cache] = transposed(vc_ref[:, head, :])

        k_chunk = lambda j, c: k_all[j, c * kc:(c + 1) * kc, :]
        vt_chunk = lambda j, c: vt_all[j, :, c * kc:(c + 1) * kc]
    else:
        vts = [transposed(vn_ref[j]) for j in range(hb)]
        k_chunk = lambda j, c: kn_ref[j, c * kc:(c + 1) * kc, :]
        vt_chunk = lambda j, c: vts[j][:, c * kc:(c + 1) * kc]

    def q_columns(j):
        if diff:
            qt = q_ref[j].astype(F32).T
            row = lax.broadcasted_iota(jnp.int32, qt.shape, 0)
            qt = jnp.concatenate([jnp.where(row < D_A, qt, 0.0), jnp.where(row >= D_A, qt, 0.0)], axis=1)
        else:
            qt = jnp.concatenate([q_ref[2 * j].astype(F32).T, q_ref[2 * j + 1].astype(F32).T], axis=1)
        return qt.astype(BF16)

    qts = [q_columns(j) for j in range(hb)]

    def scores(item):
        j, c = item
        return jnp.dot(k_chunk(j, c), qts[j], preferred_element_type=F32)

    if diff:
        lam = (jnp.exp(jnp.sum(lq1[...] * lk1[...], axis=-1, keepdims=True))
               - jnp.exp(jnp.sum(lq2[...] * lk2[...], axis=-1, keepdims=True)) + lam_init)

    def finish(j, l, acc):
        inv = 1.0 / l
        if diff:
            o = acc[:, :tq] * inv[:, :tq] - acc[:, tq:] * (lam * inv[:, tq:])
            o = _rms(o.T, g_ref[...]) * (1.0 - lam_init)
            o_ref[j] = o.astype(BF16)
        else:
            o = acc * inv
            o_ref[2 * j] = o[:, :tq].T.astype(BF16)
            o_ref[2 * j + 1] = o[:, tq:].T.astype(BF16)

    n_chunks = n_keys // kc
    items = [(j, c) for j in range(hb) for c in range(n_chunks)]
    m = l = acc = None
    s_next = scores(items[0])
    for idx, (j, c) in enumerate(items):
        s = s_next
        if idx + 1 < len(items):
            s_next = scores(items[idx + 1])
        m_c = jnp.max(s, axis=0, keepdims=True)
        m_new = m_c if c == 0 else jnp.maximum(m, m_c)
        e = jnp.exp2(s - m_new)
        l_c = jnp.sum(e, axis=0, keepdims=True)
        pv = jnp.dot(vt_chunk(j, c), e.astype(BF16), preferred_element_type=F32)
        if c == 0:
            l, acc = l_c, pv
        else:
            alpha = jnp.exp2(m - m_new)
            l = alpha * l + l_c
            acc = alpha * acc + pv
        m = m_new
        if c == n_chunks - 1:
            finish(j, l, acc)


def _attention(q, k_new, v_new, *, diff, n_batch, tok_per_batch, tq, layer, lam_init=0.0,
               lam_params=None, subln_g=None, cache_k=None, cache_v=None):
    n_kv = k_new.shape[0]
    n = q.shape[1]
    nq = tok_per_batch // tq
    n_cache = 0 if cache_k is None else cache_k.shape[2]
    n_keys = n_cache + tok_per_batch
    kc = math.gcd(n_keys, KEY_CHUNK)
    hb = HEADS_PER_STEP_CACHED if n_cache else n_kv

    in_specs, args = [], []
    if diff:
        in_specs += [pl.BlockSpec((1, D_A), lambda b, h, i: (0, 0))] * 4
        args += list(lam_params)
        in_specs.append(pl.BlockSpec((1, HEAD_W), lambda b, h, i: (0, 0)))
        args.append(subln_g)
    q_block = (hb if diff else 2 * hb, tq, HEAD_W)
    in_specs += [
        pl.BlockSpec(q_block, lambda b, h, i: (h, b * nq + i, 0)),
        pl.BlockSpec((hb, tok_per_batch, HEAD_W), lambda b, h, i: (h, b, 0)),
        pl.BlockSpec((hb, tok_per_batch, HEAD_W), lambda b, h, i: (h, b, 0)),
    ]
    args += [q, k_new, v_new]
    scratch = []
    if n_cache:
        cspec = pl.BlockSpec((None, None, n_cache, n_kv, HEAD_W), lambda b, h, i: (b, layer, 0, 0, 0))
        in_specs += [cspec, cspec]
        args += [cache_k, cache_v]
        scratch = [pltpu.VMEM((hb, n_keys, HEAD_W), BF16), pltpu.VMEM((hb, HEAD_W, n_keys), BF16)]
    return pl.pallas_call(
        functools.partial(_attn_kernel, diff=diff, n_cache=n_cache, n_new=tok_per_batch, tq=tq, kc=kc,
                          hb=hb, lam_init=lam_init),
        grid=(n_batch, n_kv // hb, nq),
        in_specs=in_specs,
        out_specs=pl.BlockSpec(q_block, lambda b, h, i: (h, b * nq + i, 0)),
        out_shape=jax.ShapeDtypeStruct((H_A, n, HEAD_W), BF16),
        scratch_shapes=scratch,
        compiler_params=_params(3),
        name=("attn_diff" if diff else "attn_gqa") + ("_cache" if n_cache else ""),
    )(*args)


def _layer_norm(y, g, b):
    mu = jnp.mean(y, axis=-1, keepdims=True)
    yc = y - mu
    var = jnp.mean(yc * yc, axis=-1, keepdims=True)
    return yc * lax.rsqrt(var + LN_EPS) * g + b


def _route(h2, wr_ref, rb_ref):
    logits = jnp.dot(h2, wr_ref[...], preferred_element_type=F32)
    scores = jax.nn.sigmoid(logits)
    biased = scores + rb_ref[...]
    lane = lax.broadcasted_iota(jnp.int32, biased.shape, 1)
    valid = lane < N_EXPERTS
    pos = lane & (EXPERTS_PER_GROUP - 1)

    def neighbour(kk):
        fwd = pltpu.roll(biased, LANES - kk, 1)
        bwd = pltpu.roll(biased, EXPERTS_PER_GROUP - kk, 1)
        wrapped = pos + kk >= EXPERTS_PER_GROUP
        return jnp.where(wrapped, bwd, fwd), wrapped

    (v1, w1), (v2, w2), (v3, w3) = neighbour(1), neighbour(2), neighbour(3)
    rank = jnp.zeros(biased.shape, jnp.int32)
    for v, wrapped in ((v1, w1), (v2, w2), (v3, w3)):
        ge = jnp.where(v >= biased, 1, 0)
        gt = jnp.where(v > biased, 1, 0)
        rank = rank + jnp.where(wrapped, ge, gt)
    hi_a, lo_a = jnp.maximum(biased, v1), jnp.minimum(biased, v1)
    hi_b, lo_b = jnp.maximum(v2, v3), jnp.minimum(v2, v3)
    top1 = jnp.maximum(hi_a, hi_b)
    top2 = jnp.maximum(jnp.minimum(hi_a, hi_b), jnp.maximum(lo_a, lo_b))
    grp_score = jnp.where(valid, top1 + top2, -jnp.inf)
    best = jnp.max(grp_score, axis=-1, keepdims=True)
    grp = (lane // EXPERTS_PER_GROUP).astype(F32)
    first = jnp.min(jnp.where(grp_score == best, grp, float(LANES)), axis=-1, keepdims=True)
    chosen = jnp.where(grp == first, rank, EXPERTS_PER_GROUP) < 2
    w_raw = jnp.where(chosen, scores, 0.0)
    return w_raw / jnp.sum(w_raw, axis=-1, keepdims=True)


def _post_kernel(x_ref, mod_ref, oa_ref, ob_ref, wga_ref, wgb_ref, wa_ref, wb_ref, wo_ref, lg_ref, lb_ref,
                 wr_ref, rb_ref, x1_ref, h2_ref, comb_ref):
    sh1 = mod_ref[:, 0:D_MODEL]
    sc1 = mod_ref[:, D_MODEL:2 * D_MODEL]
    g1 = mod_ref[:, 2 * D_MODEL:3 * D_MODEL]
    sh2 = mod_ref[:, 3 * D_MODEL:4 * D_MODEL]
    sc2 = mod_ref[:, 4 * D_MODEL:5 * D_MODEL]
    tm = x_ref.shape[0]
    rb = min(tm, POST_ROWS)
    blocks = [slice(r, r + rb) for r in range(0, tm, rb)]

    def branch_dots(rows):
        h = (x_ref[rows, :] * (1.0 + sc1) + sh1).astype(BF16)
        oa = jnp.concatenate([oa_ref[i, rows, :] for i in range(H_A)], axis=1)
        ob = jnp.concatenate([ob_ref[i, rows, :] for i in range(H_B)], axis=1)
        return (jnp.dot(h, wga_ref[...], preferred_element_type=F32),
                jnp.dot(oa, wa_ref[...], preferred_element_type=F32),
                jnp.dot(h, wgb_ref[...], preferred_element_type=F32),
                jnp.dot(ob, wb_ref[...], preferred_element_type=F32))

    def mix_dot(d):
        ga, ba, gb, bb = d
        merged = jax.nn.sigmoid(ga) * ba + jax.nn.sigmoid(gb) * bb
        return jnp.dot(merged.astype(BF16), wo_ref[...], preferred_element_type=F32)

    def norm_and_route(rows, mix):
        x1 = _layer_norm(ALPHA * x_ref[rows, :] + g1 * mix, lg_ref[...], lb_ref[...])
        x1_ref[rows, :] = x1
        h2 = (x1 * (1.0 + sc2) + sh2).astype(BF16)
        h2_ref[rows, :] = h2
        comb_ref[rows, :] = _route(h2, wr_ref, rb_ref)

    dots = [branch_dots(rows) for rows in blocks]
    mixes = [mix_dot(d) for d in dots]
    for rows, mix in zip(blocks, mixes):
        norm_and_route(rows, mix)


def _post(x, mod_rows, oa, ob, w_in_bf, wa_bf, wb_bf, wo_bf, ln_g, ln_b, wr_pad, rb_pad, layer, *,
          tok_per_batch, latent):
    n = x.shape[0]
    tm = min(TM_POST, tok_per_batch if latent else n)
    tiles_per_batch = max(tok_per_batch // tm, 1)
    const2 = lambda i: (0, 0)
    once = dict(pipeline_mode=pl.Buffered(1))
    wspec = pl.BlockSpec((None, D_MODEL, D_MODEL), lambda i: (layer, 0, 0), **once)
    return pl.pallas_call(
        _post_kernel,
        grid=(n // tm,),
        in_specs=[
            pl.BlockSpec((tm, D_MODEL), lambda i: (i, 0)),
            pl.BlockSpec((None, 1, 6 * D_MODEL), lambda i: (1 + i // tiles_per_batch if latent else 0, 0, 0)),
            pl.BlockSpec((H_A, tm, HEAD_W), lambda i: (0, i, 0)),
            pl.BlockSpec((H_B, tm, HEAD_W), lambda i: (0, i, 0)),
            pl.BlockSpec((None, D_MODEL, D_MODEL), lambda i: (layer, 0, W_QKV // D_MODEL), **once),
            pl.BlockSpec((None, D_MODEL, D_MODEL), lambda i: (layer, 0, W_QKV // D_MODEL + 1), **once),
            wspec, wspec, wspec,
            pl.BlockSpec((1, D_MODEL), const2),
            pl.BlockSpec((1, D_MODEL), const2),
            pl.BlockSpec((D_MODEL, LANES), const2),
            pl.BlockSpec((1, LANES), const2),
        ],
        out_specs=[
            pl.BlockSpec((tm, D_MODEL), lambda i: (i, 0)),
            pl.BlockSpec((tm, D_MODEL), lambda i: (i, 0)),
            pl.BlockSpec((tm, LANES), lambda i: (i, 0)),
        ],
        out_shape=[
            jax.ShapeDtypeStruct((n, D_MODEL), F32),
            jax.ShapeDtypeStruct((n, D_MODEL), BF16),
            jax.ShapeDtypeStruct((n, LANES), F32),
        ],
        compiler_params=_params(1),
        name="post_attn",
    )(x, mod_rows, oa, ob, w_in_bf, w_in_bf, wa_bf, wb_bf, wo_bf, ln_g, ln_b, wr_pad, rb_pad)


def _one_hot(idx, a, b):
    return jnp.where(idx == a, 1.0, jnp.where(idx == b, 1.0, 0.0)).astype(BF16)


def _moe_kernel(x1_ref, h2_ref, comb_ref, mod_ref, tri_ref, wg_ref, wu_ref, wd_ref, lg_ref, lb_ref, o_ref,
                hs_ref, cs_ref, ys_ref, pos_ref, meta_ref, *, tm, n_rows, perm, unperm):
    i, step = pl.program_id(0), pl.program_id(1)
    n_alloc = hs_ref.shape[0]

    @pl.when(step == 0)
    def _():
        @pl.when(i == 0)
        def _():
            hs_ref[n_rows:n_alloc, :] = jnp.zeros((n_alloc - n_rows, D_MODEL), BF16)
            cs_ref[n_rows:n_alloc, :] = jnp.zeros((n_alloc - n_rows, LANES), F32)
            ys_ref[...] = jnp.zeros_like(ys_ref)

        comb = comb_ref[...]
        sel = comb > 0.0
        lane = lax.broadcasted_iota(jnp.int32, comb.shape, 1)
        ones = jnp.where(sel, 1.0, 0.0).astype(BF16)
        cum = jnp.dot(tri_ref[...], ones, preferred_element_type=F32)
        cnt = jnp.broadcast_to(cum[tm - 1:tm, :], (8, LANES)).astype(jnp.int32)
        padded = (cnt + (SORT_ALIGN - 1)) & -SORT_ALIGN
        lane8 = lax.broadcasted_iota(jnp.int32, padded.shape, 1)
        incl = padded
        for d in (1, 2, 4, 8):
            incl = incl + jnp.where(lane8 >= d, pltpu.roll(incl, d, 1), 0)
        off = incl - padded
        for k in range(N_EXPERTS):
            meta_ref[0, k] = off[0, k]
            meta_ref[1, k] = cnt[0, k]

        pos = off[0:1, :].astype(F32) + cum - 1.0
        p_lo = jnp.min(jnp.where(sel, pos, float(n_alloc)), axis=1, keepdims=True)
        p_hi = jnp.max(jnp.where(sel, pos, -1.0), axis=1, keepdims=True)
        pos_cols = jnp.where(lane == 0, p_lo, jnp.where(lane == 1, p_hi, 0.0))
        pos_ref[...] = pos_cols
        pos_rows = pos_cols.T
        r_lo, r_hi = pos_rows[0:1, :], pos_rows[1:2, :]

        c0 = comb.astype(BF16).astype(F32)
        c1 = (comb - c0).astype(BF16).astype(F32)
        c2 = (comb - c0 - c1).astype(BF16).astype(F32)
        pieces = (c0 + pltpu.roll(c1, N_EXPERTS, 1) + pltpu.roll(c2, 2 * N_EXPERTS, 1)).astype(BF16)
        h2 = h2_ref[...]
        for rc in range(n_rows // perm):
            lo = rc * perm
            ridx = (lax.broadcasted_iota(jnp.int32, (perm, tm), 0) + lo).astype(F32)
            p = _one_hot(ridx, r_lo, r_hi)
            hs_ref[lo:lo + perm, :] = jnp.dot(p, h2, preferred_element_type=F32).astype(BF16)
            c3 = jnp.dot(p, pieces, preferred_element_type=F32)
            cs_ref[lo:lo + perm, :] = (c3 + pltpu.roll(c3, LANES - N_EXPERTS, 1)
                                       + pltpu.roll(c3, LANES - 2 * N_EXPERTS, 1))

    lane_c = lax.broadcasted_iota(jnp.int32, (MOE_CHUNK, LANES), 1)
    for k in range(EXPERTS_PER_STEP):
        e = step * EXPERTS_PER_STEP + k
        off_e, cnt_e = meta_ref[0, e], meta_ref[1, e]

        def chunk(j, carry, k=k, e=e, off_e=off_e):
            start = pl.multiple_of(off_e + j * MOE_CHUNK, SORT_ALIGN)
            rows = hs_ref[pl.ds(start, MOE_CHUNK), :]
            w = jnp.sum(jnp.where(lane_c == e, cs_ref[pl.ds(start, MOE_CHUNK), :], 0.0), axis=1, keepdims=True)
            g = jnp.dot(rows, wg_ref[k], preferred_element_type=F32)
            u = jnp.dot(rows, wu_ref[k], preferred_element_type=F32)
            act = (g * jax.nn.sigmoid(g)) * u * w
            y = jnp.dot(act.astype(BF16), wd_ref[k], preferred_element_type=F32)
            ys_ref[pl.ds(start, MOE_CHUNK), :] = y.astype(BF16)
            return carry

        lax.fori_loop(0, (cnt_e + (MOE_CHUNK - 1)) // MOE_CHUNK, chunk, 0)

    @pl.when(step == N_EXPERTS // EXPERTS_PER_STEP - 1)
    def _():
        p_lo, p_hi = pos_ref[:, 0:1], pos_ref[:, 1:2]
        y = None
        for cc in range(n_rows // unperm):
            lo = cc * unperm
            cidx = (lax.broadcasted_iota(jnp.int32, (tm, unperm), 1) + lo).astype(F32)
            q = _one_hot(cidx, p_lo, p_hi)
            part = jnp.dot(q, ys_ref[lo:lo + unperm, :], preferred_element_type=F32)
            y = part if y is None else y + part
        g2 = mod_ref[:, 5 * D_MODEL:6 * D_MODEL]
        o_ref[...] = _layer_norm(ALPHA * x1_ref[...] + g2 * y, lg_ref[...], lb_ref[...])


def _moe(x1, h2, comb, mod_rows, w_gate, w_up, w_down, ln_g, ln_b, layer, *, tok_per_batch, latent):
    n = x1.shape[0]
    tm = min(TM_MOE, tok_per_batch if latent else n)
    tiles_per_batch = max(tok_per_batch // tm, 1)
    n_rows = TOP_K * tm + N_EXPERTS * SORT_ALIGN
    perm = math.gcd(n_rows, PERM_CHUNK)
    unperm = math.gcd(n_rows, UNPERM_CHUNK)
    n_alloc = n_rows + MOE_CHUNK
    tri = jnp.tril(jnp.ones((tm, tm), BF16))
    const2 = lambda i, e: (0, 0)
    once = dict(pipeline_mode=pl.Buffered(1))
    return pl.pallas_call(
        functools.partial(_moe_kernel, tm=tm, n_rows=n_rows, perm=perm, unperm=unperm),
        grid=(n // tm, N_EXPERTS // EXPERTS_PER_STEP),
        in_specs=[
            pl.BlockSpec((tm, D_MODEL), lambda i, e: (i, 0), **once),
            pl.BlockSpec((tm, D_MODEL), lambda i, e: (i, 0), **once),
            pl.BlockSpec((tm, LANES), lambda i, e: (i, 0), **once),
            pl.BlockSpec((None, 1, 6 * D_MODEL),
                         lambda i, e: (1 + i // tiles_per_batch if latent else 0, 0, 0)),
            pl.BlockSpec((tm, tm), const2, **once),
            pl.BlockSpec((None, EXPERTS_PER_STEP, D_MODEL, D_EXPERT), lambda i, e: (layer, e, 0, 0)),
            pl.BlockSpec((None, EXPERTS_PER_STEP, D_MODEL, D_EXPERT), lambda i, e: (layer, e, 0, 0)),
            pl.BlockSpec((None, EXPERTS_PER_STEP, D_EXPERT, D_MODEL), lambda i, e: (layer, e, 0, 0)),
            pl.BlockSpec((1, D_MODEL), const2),
            pl.BlockSpec((1, D_MODEL), const2),
        ],
        out_specs=pl.BlockSpec((tm, D_MODEL), lambda i, e: (i, 0)),
        out_shape=jax.ShapeDtypeStruct((n, D_MODEL), F32),
        scratch_shapes=[
            pltpu.VMEM((n_alloc, D_MODEL), BF16),
            pltpu.VMEM((n_alloc, LANES), F32),
            pltpu.VMEM((n_alloc, D_MODEL), BF16),
            pltpu.VMEM((tm, LANES), F32),
            pltpu.SMEM((2, N_EXPERTS), jnp.int32),
        ],
        compiler_params=_params(2),
        name="moe_ln2",
    )(x1, h2, comb, mod_rows, tri, w_gate, w_up, w_down, ln_g, ln_b)


def _lambda_init(layer):
    return 0.8 - 0.6 * math.exp(-0.3 * layer)


def kernel(x_prompt, x_sample, cache_a_k, cache_a_v, cache_b_k, cache_b_v, c, c_ctx, w_mod, b_mod, w_in,
           lam_q1, lam_k1, lam_q2, lam_k2, subln_g, qn_g, kn_g, w_br_a, w_br_b, w_out, ln1_g, ln1_b,
           ln2_g, ln2_b, w_router, router_bias, w_gate, w_up, w_down):
    batch, seq, _ = x_prompt.shape
    dec_batch, dec_seq, _ = x_sample.shape

    cond8 = jnp.zeros((8, D_MODEL), F32).at[0].set(c_ctx).at[1:1 + dec_batch].set(c)
    mod = _modulation(cond8, w_mod, b_mod).reshape(DEPTH, 8, 1, 6 * D_MODEL)

    w_in_bf = w_in.astype(BF16)
    wa_bf, wb_bf, wo_bf = w_br_a.astype(BF16), w_br_b.astype(BF16), w_out.astype(BF16)
    wg_bf, wu_bf, wd_bf = w_gate.astype(BF16), w_up.astype(BF16), w_down.astype(BF16)
    wr_pad = jnp.zeros((D_MODEL, LANES), BF16).at[:, :N_EXPERTS].set(w_router.astype(BF16))
    rb_pad = jnp.zeros((1, LANES), F32).at[0, :N_EXPERTS].set(router_bias)
    rope_tabs = _rope_tables(dec_seq, D_A) + _rope_tables(dec_seq, D_B)
    row = lambda a, l: a[l].reshape(1, -1)
    cak, cav, cbk, cbv = cache_a_k, cache_a_v, cache_b_k, cache_b_v

    def layer_step(x, l, state_bufs, latent):
        tok = dec_seq if latent else seq
        nb = dec_batch if latent else batch
        outs = _qkv(x, mod[l], w_in_bf, row(qn_g, l), row(kn_g, l), l, tok_per_batch=tok,
                    rope_tabs=rope_tabs if latent else None, state_bufs=state_bufs)
        qa, ka, va, qb, kb, vb = outs[:6]
        tq = min(TQ_ATTN, tok)
        oa = _attention(qa, ka, va, diff=True, n_batch=nb, tok_per_batch=tok, tq=tq, layer=l,
                        lam_init=_lambda_init(l), subln_g=row(subln_g, l),
                        lam_params=tuple(row(a, l) for a in (lam_q1, lam_k1, lam_q2, lam_k2)),
                        cache_k=cak if latent else None, cache_v=cav if latent else None)
        ob = _attention(qb, kb, vb, diff=False, n_batch=nb, tok_per_batch=tok, tq=tq, layer=l,
                        cache_k=cbk if latent else None, cache_v=cbv if latent else None)
        x1, h2, comb = _post(x, mod[l], oa, ob, w_in_bf, wa_bf, wb_bf, wo_bf, row(ln1_g, l), row(ln1_b, l),
                             wr_pad, rb_pad, l, tok_per_batch=tok, latent=latent)
        x2 = _moe(x1, h2, comb, mod[l], wg_bf, wu_bf, wd_bf, row(ln2_g, l), row(ln2_b, l), l,
                  tok_per_batch=tok, latent=latent)
        return x2, (outs[6:] if state_bufs is not None else None)

    wa_state = H_A * HEAD_W
    wb_state = G_B * HEAD_W
    state = tuple(jnp.zeros((batch, DEPTH, seq, w), F32) for w in (wa_state, wa_state, wb_state, wb_state))
    y = x_prompt.reshape(batch * seq, D_MODEL)
    for l in range(DEPTH):
        y, state = layer_step(y, l, state, latent=False)
    y_prompt = y.reshape(batch, seq, D_MODEL)

    y = x_sample.reshape(dec_batch * dec_seq, D_MODEL)
    for l in range(DEPTH):
        y, _ = layer_step(y, l, None, latent=True)
    y_sample = y.reshape(dec_batch, dec_seq, D_MODEL)

    sak, sav, sbk, sbv = state
    return (y_prompt, y_sample,
            sak.reshape(batch, DEPTH, seq, H_A, 2 * D_A), sav.reshape(batch, DEPTH, seq, H_A, 2 * D_A),
            sbk.reshape(batch, DEPTH, seq, G_B, D_B), sbv.reshape(batch, DEPTH, seq, G_B, D_B))
```

```python
import functools
import math

import jax
import jax.numpy as jnp
from jax import lax
from jax.experimental import pallas as pl
from jax.experimental.pallas import tpu as pltpu

D_MODEL = 1024
DEPTH = 4
GRID_W = 64
ROPE_THETA = 10000.0
H_A = 8
D_A = 64
H_B = 8
G_B = 4
D_B = 128
HEAD_W = 128
N_EXPERTS = 16
EXPERTS_PER_GROUP = 4
D_EXPERT = 512
ALPHA = (2 * DEPTH) ** 0.25
LN_EPS = 1e-6
RMS_EPS = 1e-6
W_QKV = 3 * H_A * 2 * D_A + H_B * D_B + 2 * G_B * D_B
LOG2E = 1.4426950408889634
LANES = 128
VMEM_LIMIT = 56 * 1024 * 1024

BF16 = jnp.bfloat16
F32 = jnp.float32

TM_QKV = 256
TM_POST = 512
POST_ROWS = 256
TM_MOE = 1024
TN_MOD = 1536
TOP_K = 2
EXPERTS_PER_STEP = 2
SORT_ALIGN = 16
MOE_SUB = 512
MOE_CHUNK = 96
PERM_CHUNK = 256
UNPERM_CHUNK = 768
TQ_ATTN = 1024
HEADS_PER_STEP_CACHED = 2
KEY_CHUNK = 512


def _params(n_axes, vmem=VMEM_LIMIT):
    return pltpu.CompilerParams(dimension_semantics=("arbitrary",) * n_axes,
                                vmem_limit_bytes=vmem)


def _mod_kernel(c_ref, w_ref, b_ref, o_ref):
    c = c_ref[...]
    s = (c * jax.nn.sigmoid(c)).astype(BF16)
    o_ref[...] = jnp.dot(s, w_ref[...].astype(BF16), preferred_element_type=F32) + b_ref[...]


def _modulation(cond8, w_mod, b_mod):
    n_out = 6 * D_MODEL
    return pl.pallas_call(
        _mod_kernel,
        grid=(DEPTH, n_out // TN_MOD),
        in_specs=[
            pl.BlockSpec((8, D_MODEL), lambda l, j: (0, 0)),
            pl.BlockSpec((None, D_MODEL, TN_MOD), lambda l, j: (l, 0, j)),
            pl.BlockSpec((None, 1, TN_MOD), lambda l, j: (l, 0, j)),
        ],
        out_specs=pl.BlockSpec((None, 8, TN_MOD), lambda l, j: (l, 0, j)),
        out_shape=jax.ShapeDtypeStruct((DEPTH, 8, n_out), F32),
        compiler_params=_params(2),
        name="modulation",
    )(cond8, w_mod, b_mod.reshape(DEPTH, 1, n_out))


def _rope_tables(n_tok, d):
    nf = d // 4
    n_rows = n_tok // GRID_W
    row = jnp.repeat(jnp.arange(n_rows, dtype=F32), GRID_W)
    col = jnp.tile(jnp.arange(GRID_W, dtype=F32), n_rows)
    inv = ROPE_THETA ** (-jnp.arange(nf, dtype=F32) / nf)
    lane = jnp.arange(LANES) % d
    axis = lane // (2 * nf)
    half = (lane % (2 * nf)) // nf
    freq = lane % nf
    pos = jnp.where(axis[None, :] == 0, row[:, None], col[:, None])
    ang = pos * inv[freq][None, :]
    cos, sin = jnp.cos(ang), jnp.sin(ang)
    sin_lo = jnp.where(half[None, :] == 0, -sin, 0.0)
    sin_hi = jnp.where(half[None, :] == 1, sin, 0.0)
    return cos, sin_lo, sin_hi


def _rope(x, cos, sin_lo, sin_hi, nf):
    up = pltpu.roll(x, LANES - nf, 1)
    down = pltpu.roll(x, nf, 1)
    return x * cos + up * sin_lo + down * sin_hi


def _rms(x, g):
    ms = jnp.mean(x * x, axis=-1, keepdims=True)
    return x * lax.rsqrt(ms + RMS_EPS) * g


def _qkv_kernel(*refs, rope, states):
    x_ref, mod_ref, w_ref, qn_ref, kn_ref = refs[:5]
    pos = 5
    if rope:
        ca, sa_lo, sa_hi, cb, sb_lo, sb_hi = (r[...] for r in refs[pos:pos + 6])
        pos += 6
    if states:
        pos += 4
    qa_ref, ka_ref, va_ref, qb_ref, kb_ref, vb_ref = refs[pos:pos + 6]
    pos += 6
    if states:
        sak_ref, sav_ref, sbk_ref, sbv_ref = refs[pos:pos + 4]

    sh1 = mod_ref[:, 0:D_MODEL]
    sc1 = mod_ref[:, D_MODEL:2 * D_MODEL]
    h = (x_ref[...] * (1.0 + sc1) + sh1).astype(BF16)

    def seg(lo, width):
        return jnp.dot(h, w_ref[:, lo:lo + width], preferred_element_type=F32)

    def heads(p, n):
        return [p[:, i * HEAD_W:(i + 1) * HEAD_W] for i in range(n)]

    qa_scale = (D_A ** -0.5) * LOG2E
    qb_scale = (D_B ** -0.5) * LOG2E
    nfa, nfb = D_A // 4, D_B // 4

    p = seg(0, H_A * HEAD_W)
    for i, c in enumerate(heads(p, H_A)):
        if rope:
            c = _rope(c, ca, sa_lo, sa_hi, nfa)
        qa_ref[i] = (c * qa_scale).astype(BF16)

    p = seg(H_A * HEAD_W, H_A * HEAD_W)
    if states:
        sak_ref[...] = p
    for i, c in enumerate(heads(p, H_A)):
        if rope:
            c = _rope(c, ca, sa_lo, sa_hi, nfa)
        ka_ref[i] = c.astype(BF16)

    p = seg(2 * H_A * HEAD_W, H_A * HEAD_W)
    if states:
        sav_ref[...] = p
    for i, c in enumerate(heads(p, H_A)):
        va_ref[i] = c.astype(BF16)

    off = 3 * H_A * HEAD_W
    p = seg(off, H_B * HEAD_W)
    qn = qn_ref[...]
    for i, c in enumerate(heads(p, H_B)):
        c = _rms(c, qn)
        if rope:
            c = _rope(c, cb, sb_lo, sb_hi, nfb)
        qb_ref[i] = (c * qb_scale).astype(BF16)

    off += H_B * HEAD_W
    p = seg(off, G_B * HEAD_W)
    kn = kn_ref[...]
    for i, c in enumerate(heads(p, G_B)):
        c = _rms(c, kn)
        if states:
            sbk_ref[:, i * HEAD_W:(i + 1) * HEAD_W] = c
        if rope:
            c = _rope(c, cb, sb_lo, sb_hi, nfb)
        kb_ref[i] = c.astype(BF16)

    off += G_B * HEAD_W
    p = seg(off, G_B * HEAD_W)
    if states:
        sbv_ref[...] = p
    for i, c in enumerate(heads(p, G_B)):
        vb_ref[i] = c.astype(BF16)


def _qkv(x, mod_rows, w_in_bf, qn_g, kn_g, layer, *, tok_per_batch, rope_tabs=None, state_bufs=None):
    n = x.shape[0]
    tm = min(TM_QKV, tok_per_batch)
    tiles_per_batch = tok_per_batch // tm
    rope = rope_tabs is not None
    states = state_bufs is not None
    row0 = 1 if rope else 0

    in_specs = [
        pl.BlockSpec((tm, D_MODEL), lambda i: (i, 0)),
        pl.BlockSpec((None, 1, 6 * D_MODEL), lambda i: (row0 + i // tiles_per_batch if rope else 0, 0, 0)),
        pl.BlockSpec((None, D_MODEL, W_QKV), lambda i: (layer, 0, 0), pipeline_mode=pl.Buffered(1)),
        pl.BlockSpec((1, HEAD_W), lambda i: (0, 0)),
        pl.BlockSpec((1, HEAD_W), lambda i: (0, 0)),
    ]
    args = [x, mod_rows, w_in_bf, qn_g, kn_g]
    if rope:
        in_specs += [pl.BlockSpec((tm, LANES), lambda i: (i % tiles_per_batch, 0))] * 6
        args += list(rope_tabs)
    aliases = {}
    if states:
        for k, buf in enumerate(state_bufs):
            aliases[len(args)] = 6 + k
            in_specs.append(pl.BlockSpec(memory_space=pl.ANY))
            args.append(buf)

    def hm(nh):
        return (pl.BlockSpec((nh, tm, HEAD_W), lambda i: (0, i, 0)),
                jax.ShapeDtypeStruct((nh, n, HEAD_W), BF16))

    outs = [hm(H_A), hm(H_A), hm(H_A), hm(H_B), hm(G_B), hm(G_B)]
    if states:
        assert tm == tok_per_batch
        for buf in state_bufs:
            outs.append((pl.BlockSpec((None, None, tm, buf.shape[-1]), lambda i: (i, layer, 0, 0)),
                         jax.ShapeDtypeStruct(buf.shape, F32)))
    return pl.pallas_call(
        functools.partial(_qkv_kernel, rope=rope, states=states),
        grid=(n // tm,),
        in_specs=in_specs,
        out_specs=[o[0] for o in outs],
        out_shape=[o[1] for o in outs],
        input_output_aliases=aliases,
        compiler_params=_params(1),
        name="qkv_states" if states else "qkv_rope",
    )(*args)


def _attn_kernel(*refs, diff, n_cache, n_new, tq, kc, hb, lam_init):
    if diff:
        lq1, lk1, lq2, lk2, g_ref = refs[:5]
        refs = refs[5:]
    q_ref, kn_ref, vn_ref = refs[:3]
    refs = refs[3:]
    if n_cache:
        kc_ref, vc_ref = refs[:2]
        refs = refs[2:]
    o_ref = refs[0]
    n_keys = n_cache + n_new

    def transposed(x):
        return x.astype(F32).T.astype(BF16)

    if n_cache:
        k_all, vt_all = refs[1:3]

        @pl.when(pl.program_id(2) == 0)
        def _():
            for j in range(hb):
                k_all[j, n_cache:n_keys, :] = kn_ref[j]
                vt_all[j, :, n_cache:n_keys] = transposed(vn_ref[j])

        for head in range(kc_ref.shape[1]):
            @pl.when((pl.program_id(2) == 0) & (pl.program_id(1) == head // hb))
            def _(head=head):
                k_all[head % hb, 0:n_cache, :] = kc_ref[:, head, :].astype(BF16)
                vt_all[head % hb, :, 0:n_cache] = transposed(vc_ref[:, head, :])

        k_chunk = lambda j, c: k_all[j, c * kc:(c + 1) * kc, :]
        vt_chunk = lambda j, c: vt_all[j, :, c * kc:(c + 1) * kc]
    else:
        vts = [transposed(vn_ref[j]) for j in range(hb)]
        k_chunk = lambda j, c: kn_ref[j, c * kc:(c + 1) * kc, :]
        vt_chunk = lambda j, c: vts[j][:, c * kc:(c + 1) * kc]

    def q_columns(j):
        if diff:
            qt = q_ref[j].astype(F32).T
            row = lax.broadcasted_iota(jnp.int32, qt.shape, 0)
            qt = jnp.concatenate([jnp.where(row < D_A, qt, 0.0), jnp.where(row >= D_A, qt, 0.0)], axis=1)
        else:
            qt = jnp.concatenate([q_ref[2 * j].astype(F32).T, q_ref[2 * j + 1].astype(F32).T], axis=1)
        return qt.astype(BF16)

    qts = [q_columns(j) for j in range(hb)]

    def scores(item):
        j, c = item
        return jnp.dot(k_chunk(j, c), qts[j], preferred_element_type=F32)

    if diff:
        lam = (jnp.exp(jnp.sum(lq1[...] * lk1[...], axis=-1, keepdims=True))
               - jnp.exp(jnp.sum(lq2[...] * lk2[...], axis=-1, keepdims=True)) + lam_init)

    def finish(j, l, acc):
        inv = 1.0 / l
        if diff:
            o = acc[:, :tq] * inv[:, :tq] - acc[:, tq:] * (lam * inv[:, tq:])
            o = _rms(o.T, g_ref[...]) * (1.0 - lam_init)
            o_ref[j] = o.astype(BF16)
        else:
            o = acc * inv
            o_ref[2 * j] = o[:, :tq].T.astype(BF16)
            o_ref[2 * j + 1] = o[:, tq:].T.astype(BF16)

    n_chunks = n_keys // kc
    items = [(j, c) for j in range(hb) for c in range(n_chunks)]
    m = l = acc = None
    s_next = scores(items[0])
    for idx, (j, c) in enumerate(items):
        s = s_next
        if idx + 1 < len(items):
            s_next = scores(items[idx + 1])
        m_c = jnp.max(s, axis=0, keepdims=True)
        m_new = m_c if c == 0 else jnp.maximum(m, m_c)
        e = jnp.exp2(s - m_new)
        l_c = jnp.sum(e, axis=0, keepdims=True)
        pv = jnp.dot(vt_chunk(j, c), e.astype(BF16), preferred_element_type=F32)
        if c == 0:
            l, acc = l_c, pv
        else:
            alpha = jnp.exp2(m - m_new)
            l = alpha * l + l_c
            acc = alpha * acc + pv
        m = m_new
        if c == n_chunks - 1:
            finish(j, l, acc)


def _attention(q, k_new, v_new, *, diff, n_batch, tok_per_batch, tq, layer, lam_init=0.0,
               lam_params=None, subln_g=None, cache_k=None, cache_v=None):
    n_kv = k_new.shape[0]
    n = q.shape[1]
    nq = tok_per_batch // tq
    n_cache = 0 if cache_k is None else cache_k.shape[2]
    n_keys = n_cache + tok_per_batch
    kc = n_keys if n_keys <= KEY_CHUNK else math.gcd(n_keys, KEY_CHUNK)
    hb = HEADS_PER_STEP_CACHED if n_cache else n_kv

    in_specs, args = [], []
    if diff:
        in_specs += [pl.BlockSpec((1, D_A), lambda b, h, i: (0, 0))] * 4
        args += list(lam_params)
        in_specs.append(pl.BlockSpec((1, HEAD_W), lambda b, h, i: (0, 0)))
        args.append(subln_g)
    q_block = (hb if diff else 2 * hb, tq, HEAD_W)
    in_specs += [
        pl.BlockSpec(q_block, lambda b, h, i: (h, b * nq + i, 0)),
        pl.BlockSpec((hb, tok_per_batch, HEAD_W), lambda b, h, i: (h, b, 0)),
        pl.BlockSpec((hb, tok_per_batch, HEAD_W), lambda b, h, i: (h, b, 0)),
    ]
    args += [q, k_new, v_new]
    scratch = []
    if n_cache:
        cspec = pl.BlockSpec((None, None, n_cache, n_kv, HEAD_W), lambda b, h, i: (b, layer, 0, 0, 0))
        in_specs += [cspec, cspec]
        args += [cache_k, cache_v]
        scratch = [pltpu.VMEM((hb, n_keys, HEAD_W), BF16), pltpu.VMEM((hb, HEAD_W, n_keys), BF16)]
    return pl.pallas_call(
        functools.partial(_attn_kernel, diff=diff, n_cache=n_cache, n_new=tok_per_batch, tq=tq, kc=kc,
                          hb=hb, lam_init=lam_init),
        grid=(n_batch, n_kv // hb, nq),
        in_specs=in_specs,
        out_specs=pl.BlockSpec(q_block, lambda b, h, i: (h, b * nq + i, 0)),
        out_shape=jax.ShapeDtypeStruct((H_A, n, HEAD_W), BF16),
        scratch_shapes=scratch,
        compiler_params=_params(3),
        name=("attn_diff" if diff else "attn_gqa") + ("_cache" if n_cache else ""),
    )(*args)


def _layer_norm(y, g, b):
    mu = jnp.mean(y, axis=-1, keepdims=True)
    yc = y - mu
    var = jnp.mean(yc * yc, axis=-1, keepdims=True)
    return yc * lax.rsqrt(var + LN_EPS) * g + b


def _route(h2, wr_ref, rb_ref):
    logits = jnp.dot(h2, wr_ref[...], preferred_element_type=F32)
    scores = jax.nn.sigmoid(logits)
    biased = scores + rb_ref[...]
    lane = lax.broadcasted_iota(jnp.int32, biased.shape, 1)
    valid = lane < N_EXPERTS
    pos = lane & (EXPERTS_PER_GROUP - 1)

    def neighbour(kk):
        fwd = pltpu.roll(biased, LANES - kk, 1)
        bwd = pltpu.roll(biased, EXPERTS_PER_GROUP - kk, 1)
        wrapped = pos + kk >= EXPERTS_PER_GROUP
        return jnp.where(wrapped, bwd, fwd), wrapped

    (v1, w1), (v2, w2), (v3, w3) = neighbour(1), neighbour(2), neighbour(3)
    rank = jnp.zeros(biased.shape, jnp.int32)
    for v, wrapped in ((v1, w1), (v2, w2), (v3, w3)):
        ge = jnp.where(v >= biased, 1, 0)
        gt = jnp.where(v > biased, 1, 0)
        rank = rank + jnp.where(wrapped, ge, gt)
    hi_a, lo_a = jnp.maximum(biased, v1), jnp.minimum(biased, v1)
    hi_b, lo_b = jnp.maximum(v2, v3), jnp.minimum(v2, v3)
    top1 = jnp.maximum(hi_a, hi_b)
    top2 = jnp.maximum(jnp.minimum(hi_a, hi_b), jnp.maximum(lo_a, lo_b))
    grp_score = jnp.where(valid, top1 + top2, -jnp.inf)
    best = jnp.max(grp_score, axis=-1, keepdims=True)
    grp = (lane // EXPERTS_PER_GROUP).astype(F32)
    first = jnp.min(jnp.where(grp_score == best, grp, float(LANES)), axis=-1, keepdims=True)
    chosen = jnp.where(grp == first, rank, EXPERTS_PER_GROUP) < 2
    w_raw = jnp.where(chosen, scores, 0.0)
    return w_raw / jnp.sum(w_raw, axis=-1, keepdims=True)


def _post_kernel(x_ref, mod_ref, oa_ref, ob_ref, wga_ref, wgb_ref, wa_ref, wb_ref, wo_ref, lg_ref, lb_ref,
                 wr_ref, rb_ref, x1_ref, h2_ref, comb_ref):
    sh1 = mod_ref[:, 0:D_MODEL]
    sc1 = mod_ref[:, D_MODEL:2 * D_MODEL]
    g1 = mod_ref[:, 2 * D_MODEL:3 * D_MODEL]
    sh2 = mod_ref[:, 3 * D_MODEL:4 * D_MODEL]
    sc2 = mod_ref[:, 4 * D_MODEL:5 * D_MODEL]
    tm = x_ref.shape[0]
    rb = min(tm, POST_ROWS)
    blocks = [slice(r, r + rb) for r in range(0, tm, rb)]

    def branch_dots(rows):
        h = (x_ref[rows, :] * (1.0 + sc1) + sh1).astype(BF16)
        oa = jnp.concatenate([oa_ref[i, rows, :] for i in range(H_A)], axis=1)
        ob = jnp.concatenate([ob_ref[i, rows, :] for i in range(H_B)], axis=1)
        return (jnp.dot(h, wga_ref[...], preferred_element_type=F32),
                jnp.dot(oa, wa_ref[...], preferred_element_type=F32),
                jnp.dot(h, wgb_ref[...], preferred_element_type=F32),
                jnp.dot(ob, wb_ref[...], preferred_element_type=F32))

    def mix_dot(d):
        ga, ba, gb, bb = d
        merged = jax.nn.sigmoid(ga) * ba + jax.nn.sigmoid(gb) * bb
        return jnp.dot(merged.astype(BF16), wo_ref[...], preferred_element_type=F32)

    def norm_and_route(rows, mix):
        x1 = _layer_norm(ALPHA * x_ref[rows, :] + g1 * mix, lg_ref[...], lb_ref[...])
        x1_ref[rows, :] = x1
        h2 = (x1 * (1.0 + sc2) + sh2).astype(BF16)
        h2_ref[rows, :] = h2
        comb_ref[rows, :] = _route(h2, wr_ref, rb_ref)

    dots = [branch_dots(rows) for rows in blocks]
    mixes = [mix_dot(d) for d in dots]
    for rows, mix in zip(blocks, mixes):
        norm_and_route(rows, mix)


def _post(x, mod_rows, oa, ob, w_in_bf, wa_bf, wb_bf, wo_bf, ln_g, ln_b, wr_pad, rb_pad, layer, *,
          tok_per_batch, latent):
    n = x.shape[0]
    tm = min(TM_POST, tok_per_batch if latent else n)
    tiles_per_batch = max(tok_per_batch // tm, 1)
    const2 = lambda i: (0, 0)
    once = dict(pipeline_mode=pl.Buffered(1))
    wspec = pl.BlockSpec((None, D_MODEL, D_MODEL), lambda i: (layer, 0, 0), **once)
    return pl.pallas_call(
        _post_kernel,
        grid=(n // tm,),
        in_specs=[
            pl.BlockSpec((tm, D_MODEL), lambda i: (i, 0)),
            pl.BlockSpec((None, 1, 6 * D_MODEL), lambda i: (1 + i // tiles_per_batch if latent else 0, 0, 0)),
            pl.BlockSpec((H_A, tm, HEAD_W), lambda i: (0, i, 0)),
            pl.BlockSpec((H_B, tm, HEAD_W), lambda i: (0, i, 0)),
            pl.BlockSpec((None, D_MODEL, D_MODEL), lambda i: (layer, 0, W_QKV // D_MODEL), **once),
            pl.BlockSpec((None, D_MODEL, D_MODEL), lambda i: (layer, 0, W_QKV // D_MODEL + 1), **once),
            wspec, wspec, wspec,
            pl.BlockSpec((1, D_MODEL), const2),
            pl.BlockSpec((1, D_MODEL), const2),
            pl.BlockSpec((D_MODEL, LANES), const2),
            pl.BlockSpec((1, LANES), const2),
        ],
        out_specs=[
            pl.BlockSpec((tm, D_MODEL), lambda i: (i, 0)),
            pl.BlockSpec((tm, D_MODEL), lambda i: (i, 0)),
            pl.BlockSpec((tm, LANES), lambda i: (i, 0)),
        ],
        out_shape=[
            jax.ShapeDtypeStruct((n, D_MODEL), F32),
            jax.ShapeDtypeStruct((n, D_MODEL), BF16),
            jax.ShapeDtypeStruct((n, LANES), F32),
        ],
        compiler_params=_params(1),
        name="post_attn",
    )(x, mod_rows, oa, ob, w_in_bf, w_in_bf, wa_bf, wb_bf, wo_bf, ln_g, ln_b, wr_pad, rb_pad)


def _one_hot(idx, a, b):
    return jnp.where(idx == a, 1.0, jnp.where(idx == b, 1.0, 0.0)).astype(BF16)


def _moe_kernel(x1_ref, h2_ref, comb_ref, mod_ref, tri_ref, wg_ref, wu_ref, wd_ref, lg_ref, lb_ref, o_ref,
                hs_ref, cs_ref, ys_ref, pos_ref, meta_ref, *, sub, n_rows, perm, unperm):
    i, step = pl.program_id(0), pl.program_id(1)
    n_sub, n_alloc = hs_ref.shape[0], hs_ref.shape[1]
    tail = n_alloc - n_rows

    @pl.when(step == 0)
    def _():
        @pl.when(i == 0)
        def _():
            for s in range(n_sub):
                hs_ref[s, n_rows:n_alloc, :] = jnp.zeros((tail, D_MODEL), BF16)
                cs_ref[s, n_rows:n_alloc, :] = jnp.zeros((tail, LANES), F32)
            ys_ref[...] = jnp.zeros_like(ys_ref)

        for s in range(n_sub):
            tok = slice(s * sub, (s + 1) * sub)
            comb = comb_ref[tok, :]
            sel = comb > 0.0
            lane = lax.broadcasted_iota(jnp.int32, comb.shape, 1)
            ones = jnp.where(sel, 1.0, 0.0).astype(BF16)
            cum = jnp.dot(tri_ref[...], ones, preferred_element_type=F32)
            cnt = jnp.broadcast_to(cum[sub - 1:sub, :], (8, LANES)).astype(jnp.int32)
            padded = (cnt + (SORT_ALIGN - 1)) & -SORT_ALIGN
            lane8 = lax.broadcasted_iota(jnp.int32, padded.shape, 1)
            incl = padded
            for d in (1, 2, 4, 8):
                incl = incl + jnp.where(lane8 >= d, pltpu.roll(incl, d, 1), 0)
            off = incl - padded
            for k in range(N_EXPERTS):
                meta_ref[2 * s, k] = off[0, k]
                meta_ref[2 * s + 1, k] = cnt[0, k]

            pos = off[0:1, :].astype(F32) + cum - 1.0
            p_lo = jnp.min(jnp.where(sel, pos, float(n_alloc)), axis=1, keepdims=True)
            p_hi = jnp.max(jnp.where(sel, pos, -1.0), axis=1, keepdims=True)
            pos_cols = jnp.where(lane == 0, p_lo, jnp.where(lane == 1, p_hi, 0.0))
            pos_ref[tok, :] = pos_cols
            pos_rows = pos_cols.T
            r_lo, r_hi = pos_rows[0:1, :], pos_rows[1:2, :]

            c0 = comb.astype(BF16).astype(F32)
            c1 = (comb - c0).astype(BF16).astype(F32)
            c2 = (comb - c0 - c1).astype(BF16).astype(F32)
            pieces = (c0 + pltpu.roll(c1, N_EXPERTS, 1) + pltpu.roll(c2, 2 * N_EXPERTS, 1)).astype(BF16)
            h2 = h2_ref[tok, :]
            for rc in range(n_rows // perm):
                lo = rc * perm
                ridx = (lax.broadcasted_iota(jnp.int32, (perm, sub), 0) + lo).astype(F32)
                p = _one_hot(ridx, r_lo, r_hi)
                hs_ref[s, lo:lo + perm, :] = jnp.dot(p, h2, preferred_element_type=F32).astype(BF16)
                c3 = jnp.dot(p, pieces, preferred_element_type=F32)
                cs_ref[s, lo:lo + perm, :] = (c3 + pltpu.roll(c3, LANES - N_EXPERTS, 1)
                                              + pltpu.roll(c3, LANES - 2 * N_EXPERTS, 1))

    lane_c = lax.broadcasted_iota(jnp.int32, (n_sub * MOE_CHUNK, LANES), 1)
    for k in range(EXPERTS_PER_STEP):
        e = step * EXPERTS_PER_STEP + k
        offs = [meta_ref[2 * s, e] for s in range(n_sub)]
        n_chunks = functools.reduce(
            jnp.maximum, [(meta_ref[2 * s + 1, e] + (MOE_CHUNK - 1)) // MOE_CHUNK for s in range(n_sub)])

        def chunk(j, carry, k=k, e=e, offs=offs):
            starts = [pl.multiple_of(jnp.minimum(off + j * MOE_CHUNK, n_rows), SORT_ALIGN) for off in offs]
            rows = jnp.concatenate([hs_ref[s, pl.ds(st, MOE_CHUNK), :] for s, st in enumerate(starts)], axis=0)
            cw = jnp.concatenate([cs_ref[s, pl.ds(st, MOE_CHUNK), :] for s, st in enumerate(starts)], axis=0)
            w = jnp.sum(jnp.where(lane_c == e, cw, 0.0), axis=1, keepdims=True)
            g = jnp.dot(rows, wg_ref[k], preferred_element_type=F32)
            u = jnp.dot(rows, wu_ref[k], preferred_element_type=F32)
            act = (g * jax.nn.sigmoid(g)) * u * w
            y = jnp.dot(act.astype(BF16), wd_ref[k], preferred_element_type=F32).astype(BF16)
            for s, st in enumerate(starts):
                ys_ref[s, pl.ds(st, MOE_CHUNK), :] = y[s * MOE_CHUNK:(s + 1) * MOE_CHUNK, :]
            return carry

        lax.fori_loop(0, n_chunks, chunk, 0)

    @pl.when(step == N_EXPERTS // EXPERTS_PER_STEP - 1)
    def _():
        g2 = mod_ref[:, 5 * D_MODEL:6 * D_MODEL]
        ys = []
        for s in range(n_sub):
            tok = slice(s * sub, (s + 1) * sub)
            p_lo, p_hi = pos_ref[tok, 0:1], pos_ref[tok, 1:2]
            y = None
            for cc in range(n_rows // unperm):
                lo = cc * unperm
                cidx = (lax.broadcasted_iota(jnp.int32, (sub, unperm), 1) + lo).astype(F32)
                q = _one_hot(cidx, p_lo, p_hi)
                part = jnp.dot(q, ys_ref[s, lo:lo + unperm, :], preferred_element_type=F32)
                y = part if y is None else y + part
            ys.append(y)
        for s, y in enumerate(ys):
            tok = slice(s * sub, (s + 1) * sub)
            o_ref[tok, :] = _layer_norm(ALPHA * x1_ref[tok, :] + g2 * y, lg_ref[...], lb_ref[...])


def _moe(x1, h2, comb, mod_rows, w_gate, w_up, w_down, ln_g, ln_b, layer, *, tok_per_batch, latent):
    n = x1.shape[0]
    tm = min(TM_MOE, tok_per_batch if latent else n)
    tiles_per_batch = max(tok_per_batch // tm, 1)
    sub = min(MOE_SUB, tm)
    n_sub = tm // sub
    n_rows = TOP_K * sub + N_EXPERTS * SORT_ALIGN
    perm = math.gcd(n_rows, PERM_CHUNK)
    unperm = math.gcd(n_rows, UNPERM_CHUNK)
    n_alloc = n_rows + MOE_CHUNK
    tri = jnp.tril(jnp.ones((sub, sub), BF16))
    const2 = lambda i, e: (0, 0)
    once = dict(pipeline_mode=pl.Buffered(1))
    return pl.pallas_call(
        functools.partial(_moe_kernel, sub=sub, n_rows=n_rows, perm=perm, unperm=unperm),
        grid=(n // tm, N_EXPERTS // EXPERTS_PER_STEP),
        in_specs=[
            pl.BlockSpec((tm, D_MODEL), lambda i, e: (i, 0), **once),
            pl.BlockSpec((tm, D_MODEL), lambda i, e: (i, 0), **once),
            pl.BlockSpec((tm, LANES), lambda i, e: (i, 0), **once),
            pl.BlockSpec((None, 1, 6 * D_MODEL),
                         lambda i, e: (1 + i // tiles_per_batch if latent else 0, 0, 0)),
            pl.BlockSpec((sub, sub), const2, **once),
            pl.BlockSpec((None, EXPERTS_PER_STEP, D_MODEL, D_EXPERT), lambda i, e: (layer, e, 0, 0)),
            pl.BlockSpec((None, EXPERTS_PER_STEP, D_MODEL, D_EXPERT), lambda i, e: (layer, e, 0, 0)),
            pl.BlockSpec((None, EXPERTS_PER_STEP, D_EXPERT, D_MODEL), lambda i, e: (layer, e, 0, 0)),
            pl.BlockSpec((1, D_MODEL), const2),
            pl.BlockSpec((1, D_MODEL), const2),
        ],
        out_specs=pl.BlockSpec((tm, D_MODEL), lambda i, e: (i, 0)),
        out_shape=jax.ShapeDtypeStruct((n, D_MODEL), F32),
        scratch_shapes=[
            pltpu.VMEM((n_sub, n_alloc, D_MODEL), BF16),
            pltpu.VMEM((n_sub, n_alloc, LANES), F32),
            pltpu.VMEM((n_sub, n_alloc, D_MODEL), BF16),
            pltpu.VMEM((tm, LANES), F32),
            pltpu.SMEM((2 * n_sub, N_EXPERTS), jnp.int32),
        ],
        compiler_params=_params(2),
        name="moe_ln2",
    )(x1, h2, comb, mod_rows, tri, w_gate, w_up, w_down, ln_g, ln_b)


def _lambda_init(layer):
    return 0.8 - 0.6 * math.exp(-0.3 * layer)


def kernel(x_prompt, x_sample, cache_a_k, cache_a_v, cache_b_k, cache_b_v, c, c_ctx, w_mod, b_mod, w_in,
           lam_q1, lam_k1, lam_q2, lam_k2, subln_g, qn_g, kn_g, w_br_a, w_br_b, w_out, ln1_g, ln1_b,
           ln2_g, ln2_b, w_router, router_bias, w_gate, w_up, w_down):
    batch, seq, _ = x_prompt.shape
    dec_batch, dec_seq, _ = x_sample.shape

    cond8 = jnp.zeros((8, D_MODEL), F32).at[0].set(c_ctx).at[1:1 + dec_batch].set(c)
    mod = _modulation(cond8, w_mod, b_mod).reshape(DEPTH, 8, 1, 6 * D_MODEL)

    w_in_bf = w_in.astype(BF16)
    wa_bf, wb_bf, wo_bf = w_br_a.astype(BF16), w_br_b.astype(BF16), w_out.astype(BF16)
    wg_bf, wu_bf, wd_bf = w_gate.astype(BF16), w_up.astype(BF16), w_down.astype(BF16)
    wr_pad = jnp.zeros((D_MODEL, LANES), BF16).at[:, :N_EXPERTS].set(w_router.astype(BF16))
    rb_pad = jnp.zeros((1, LANES), F32).at[0, :N_EXPERTS].set(router_bias)
    rope_tabs = _rope_tables(dec_seq, D_A) + _rope_tables(dec_seq, D_B)
    row = lambda a, l: a[l].reshape(1, -1)
    cak, cav, cbk, cbv = cache_a_k, cache_a_v, cache_b_k, cache_b_v

    def layer_step(x, l, state_bufs, latent):
        tok = dec_seq if latent else seq
        nb = dec_batch if latent else batch
        outs = _qkv(x, mod[l], w_in_bf, row(qn_g, l), row(kn_g, l), l, tok_per_batch=tok,
                    rope_tabs=rope_tabs if latent else None, state_bufs=state_bufs)
        qa, ka, va, qb, kb, vb = outs[:6]
        tq = min(TQ_ATTN, tok)
        oa = _attention(qa, ka, va, diff=True, n_batch=nb, tok_per_batch=tok, tq=tq, layer=l,
                        lam_init=_lambda_init(l), subln_g=row(subln_g, l),
                        lam_params=tuple(row(a, l) for a in (lam_q1, lam_k1, lam_q2, lam_k2)),
                        cache_k=cak if latent else None, cache_v=cav if latent else None)
        ob = _attention(qb, kb, vb, diff=False, n_batch=nb, tok_per_batch=tok, tq=tq, layer=l,
                        cache_k=cbk if latent else None, cache_v=cbv if latent else None)
        x1, h2, comb = _post(x, mod[l], oa, ob, w_in_bf, wa_bf, wb_bf, wo_bf, row(ln1_g, l), row(ln1_b, l),
                             wr_pad, rb_pad, l, tok_per_batch=tok, latent=latent)
        x2 = _moe(x1, h2, comb, mod[l], wg_bf, wu_bf, wd_bf, row(ln2_g, l), row(ln2_b, l), l,
                  tok_per_batch=tok, latent=latent)
        return x2, (outs[6:] if state_bufs is not None else None)

    wa_state = H_A * HEAD_W
    wb_state = G_B * HEAD_W
    state = tuple(jnp.zeros((batch, DEPTH, seq, w), F32) for w in (wa_state, wa_state, wb_state, wb_state))
    y = x_prompt.reshape(batch * seq, D_MODEL)
    for l in range(DEPTH):
        y, state = layer_step(y, l, state, latent=False)
    y_prompt = y.reshape(batch, seq, D_MODEL)

    y = x_sample.reshape(dec_batch * dec_seq, D_MODEL)
    for l in range(DEPTH):
        y, _ = layer_step(y, l, None, latent=True)
    y_sample = y.reshape(dec_batch, dec_seq, D_MODEL)

    sak, sav, sbk, sbv = state
    return (y_prompt, y_sample,
            sak.reshape(batch, DEPTH, seq, H_A, 2 * D_A), sav.reshape(batch, DEPTH, seq, H_A, 2 * D_A),
            sbk.reshape(batch, DEPTH, seq, G_B, D_B), sbv.reshape(batch, DEPTH, seq, G_B, D_B))
```

```python
import functools
import math

import jax
import jax.numpy as jnp
from jax import lax
from jax.experimental import pallas as pl
from jax.experimental.pallas import tpu as pltpu

D_MODEL = 1024
DEPTH = 4
GRID_W = 64
ROPE_THETA = 10000.0
H_A = 8
D_A = 64
H_B = 8
G_B = 4
D_B = 128
HEAD_W = 128
N_EXPERTS = 16
EXPERTS_PER_GROUP = 4
D_EXPERT = 512
ALPHA = (2 * DEPTH) ** 0.25
LN_EPS = 1e-6
RMS_EPS = 1e-6
W_QKV = 3 * H_A * 2 * D_A + H_B * D_B + 2 * G_B * D_B
LOG2E = 1.4426950408889634
LANES = 128
VMEM_LIMIT = 56 * 1024 * 1024

BF16 = jnp.bfloat16
F32 = jnp.float32

TM_QKV = 256
TM_POST = 512
POST_ROWS = 256
TM_MOE = 1024
TN_MOD = 1536
TOP_K = 2
EXPERTS_PER_STEP = 2
SORT_ALIGN = 16
MOE_SUB = 256
MOE_CHUNK = 48
PERM_CHUNK = 256
UNPERM_CHUNK = 768
TQ_ATTN = 1024
HEADS_PER_STEP_CACHED = 2
KEY_CHUNK = 512


def _params(n_axes, vmem=VMEM_LIMIT):
    return pltpu.CompilerParams(dimension_semantics=("arbitrary",) * n_axes,
                                vmem_limit_bytes=vmem)


def _mod_kernel(c_ref, w_ref, b_ref, o_ref):
    c = c_ref[...]
    s = (c * jax.nn.sigmoid(c)).astype(BF16)
    o_ref[...] = jnp.dot(s, w_ref[...].astype(BF16), preferred_element_type=F32) + b_ref[...]


def _modulation(cond8, w_mod, b_mod):
    n_out = 6 * D_MODEL
    return pl.pallas_call(
        _mod_kernel,
        grid=(DEPTH, n_out // TN_MOD),
        in_specs=[
            pl.BlockSpec((8, D_MODEL), lambda l, j: (0, 0)),
            pl.BlockSpec((None, D_MODEL, TN_MOD), lambda l, j: (l, 0, j)),
            pl.BlockSpec((None, 1, TN_MOD), lambda l, j: (l, 0, j)),
        ],
        out_specs=pl.BlockSpec((None, 8, TN_MOD), lambda l, j: (l, 0, j)),
        out_shape=jax.ShapeDtypeStruct((DEPTH, 8, n_out), F32),
        compiler_params=_params(2),
        name="modulation",
    )(cond8, w_mod, b_mod.reshape(DEPTH, 1, n_out))


def _rope_tables(n_tok, d):
    nf = d // 4
    n_rows = n_tok // GRID_W
    row = jnp.repeat(jnp.arange(n_rows, dtype=F32), GRID_W)
    col = jnp.tile(jnp.arange(GRID_W, dtype=F32), n_rows)
    inv = ROPE_THETA ** (-jnp.arange(nf, dtype=F32) / nf)
    lane = jnp.arange(LANES) % d
    axis = lane // (2 * nf)
    half = (lane % (2 * nf)) // nf
    freq = lane % nf
    pos = jnp.where(axis[None, :] == 0, row[:, None], col[:, None])
    ang = pos * inv[freq][None, :]
    cos, sin = jnp.cos(ang), jnp.sin(ang)
    sin_lo = jnp.where(half[None, :] == 0, -sin, 0.0)
    sin_hi = jnp.where(half[None, :] == 1, sin, 0.0)
    return cos, sin_lo, sin_hi


def _rope(x, cos, sin_lo, sin_hi, nf):
    up = pltpu.roll(x, LANES - nf, 1)
    down = pltpu.roll(x, nf, 1)
    return x * cos + up * sin_lo + down * sin_hi


def _rms(x, g):
    ms = jnp.mean(x * x, axis=-1, keepdims=True)
    return x * lax.rsqrt(ms + RMS_EPS) * g


def _qkv_kernel(*refs, rope, states):
    x_ref, mod_ref, w_ref, qn_ref, kn_ref = refs[:5]
    pos = 5
    if rope:
        ca, sa_lo, sa_hi, cb, sb_lo, sb_hi = (r[...] for r in refs[pos:pos + 6])
        pos += 6
    if states:
        pos += 4
    qa_ref, ka_ref, va_ref, qb_ref, kb_ref, vb_ref = refs[pos:pos + 6]
    pos += 6
    if states:
        sak_ref, sav_ref, sbk_ref, sbv_ref = refs[pos:pos + 4]

    sh1 = mod_ref[:, 0:D_MODEL]
    sc1 = mod_ref[:, D_MODEL:2 * D_MODEL]
    h = (x_ref[...] * (1.0 + sc1) + sh1).astype(BF16)

    def seg(lo, width):
        return jnp.dot(h, w_ref[:, lo:lo + width], preferred_element_type=F32)

    def heads(p, n):
        return [p[:, i * HEAD_W:(i + 1) * HEAD_W] for i in range(n)]

    qa_scale = (D_A ** -0.5) * LOG2E
    qb_scale = (D_B ** -0.5) * LOG2E
    nfa, nfb = D_A // 4, D_B // 4

    p = seg(0, H_A * HEAD_W)
    for i, c in enumerate(heads(p, H_A)):
        if rope:
            c = _rope(c, ca, sa_lo, sa_hi, nfa)
        qa_ref[i] = (c * qa_scale).astype(BF16)

    p = seg(H_A * HEAD_W, H_A * HEAD_W)
    if states:
        sak_ref[...] = p
    for i, c in enumerate(heads(p, H_A)):
        if rope:
            c = _rope(c, ca, sa_lo, sa_hi, nfa)
        ka_ref[i] = c.astype(BF16)

    p = seg(2 * H_A * HEAD_W, H_A * HEAD_W)
    if states:
        sav_ref[...] = p
    for i, c in enumerate(heads(p, H_A)):
        va_ref[i] = c.astype(BF16)

    off = 3 * H_A * HEAD_W
    p = seg(off, H_B * HEAD_W)
    qn = qn_ref[...]
    for i, c in enumerate(heads(p, H_B)):
        c = _rms(c, qn)
        if rope:
            c = _rope(c, cb, sb_lo, sb_hi, nfb)
        qb_ref[i] = (c * qb_scale).astype(BF16)

    off += H_B * HEAD_W
    p = seg(off, G_B * HEAD_W)
    kn = kn_ref[...]
    for i, c in enumerate(heads(p, G_B)):
        c = _rms(c, kn)
        if states:
            sbk_ref[:, i * HEAD_W:(i + 1) * HEAD_W] = c
        if rope:
            c = _rope(c, cb, sb_lo, sb_hi, nfb)
        kb_ref[i] = c.astype(BF16)

    off += G_B * HEAD_W
    p = seg(off, G_B * HEAD_W)
    if states:
        sbv_ref[...] = p
    for i, c in enumerate(heads(p, G_B)):
        vb_ref[i] = c.astype(BF16)


def _qkv(x, mod_rows, w_in_bf, qn_g, kn_g, layer, *, tok_per_batch, rope_tabs=None, state_bufs=None):
    n = x.shape[0]
    tm = min(TM_QKV, tok_per_batch)
    tiles_per_batch = tok_per_batch // tm
    rope = rope_tabs is not None
    states = state_bufs is not None
    row0 = 1 if rope else 0

    in_specs = [
        pl.BlockSpec((tm, D_MODEL), lambda i: (i, 0)),
        pl.BlockSpec((None, 1, 6 * D_MODEL), lambda i: (row0 + i // tiles_per_batch if rope else 0, 0, 0)),
        pl.BlockSpec((None, D_MODEL, W_QKV), lambda i: (layer, 0, 0), pipeline_mode=pl.Buffered(1)),
        pl.BlockSpec((1, HEAD_W), lambda i: (0, 0)),
        pl.BlockSpec((1, HEAD_W), lambda i: (0, 0)),
    ]
    args = [x, mod_rows, w_in_bf, qn_g, kn_g]
    if rope:
        in_specs += [pl.BlockSpec((tm, LANES), lambda i: (i % tiles_per_batch, 0))] * 6
        args += list(rope_tabs)
    aliases = {}
    if states:
        for k, buf in enumerate(state_bufs):
            aliases[len(args)] = 6 + k
            in_specs.append(pl.BlockSpec(memory_space=pl.ANY))
            args.append(buf)

    def hm(nh):
        return (pl.BlockSpec((nh, tm, HEAD_W), lambda i: (0, i, 0)),
                jax.ShapeDtypeStruct((nh, n, HEAD_W), BF16))

    outs = [hm(H_A), hm(H_A), hm(H_A), hm(H_B), hm(G_B), hm(G_B)]
    if states:
        assert tm == tok_per_batch
        for buf in state_bufs:
            outs.append((pl.BlockSpec((None, None, tm, buf.shape[-1]), lambda i: (i, layer, 0, 0)),
                         jax.ShapeDtypeStruct(buf.shape, F32)))
    return pl.pallas_call(
        functools.partial(_qkv_kernel, rope=rope, states=states),
        grid=(n // tm,),
        in_specs=in_specs,
        out_specs=[o[0] for o in outs],
        out_shape=[o[1] for o in outs],
        input_output_aliases=aliases,
        compiler_params=_params(1),
        name="qkv_states" if states else "qkv_rope",
    )(*args)


def _attn_kernel(*refs, diff, n_cache, n_new, tq, kc, hb, lam_init):
    if diff:
        lq1, lk1, lq2, lk2, g_ref = refs[:5]
        refs = refs[5:]
    q_ref, kn_ref, vn_ref = refs[:3]
    refs = refs[3:]
    if n_cache:
        kc_ref, vc_ref = refs[:2]
        refs = refs[2:]
    o_ref = refs[0]
    n_keys = n_cache + n_new

    def transposed(x):
        return x.astype(F32).T.astype(BF16)

    if n_cache:
        k_all, vt_all = refs[1:3]

        @pl.when(pl.program_id(2) == 0)
        def _():
            for j in range(hb):
                k_all[j, n_cache:n_keys, :] = kn_ref[j]
                vt_all[j, :, n_cache:n_keys] = transposed(vn_ref[j])

        for head in range(kc_ref.shape[1]):
            @pl.when((pl.program_id(2) == 0) & (pl.program_id(1) == head // hb))
            def _(head=head):
                k_all[head % hb, 0:n_cache, :] = kc_ref[:, head, :].astype(BF16)
                vt_all[head % hb, :, 0:n_cache] = transposed(vc_ref[:, head, :])

        k_chunk = lambda j, c: k_all[j, c * kc:(c + 1) * kc, :]
        vt_chunk = lambda j, c: vt_all[j, :, c * kc:(c + 1) * kc]
    else:
        vts = [transposed(vn_ref[j]) for j in range(hb)]
        k_chunk = lambda j, c: kn_ref[j, c * kc:(c + 1) * kc, :]
        vt_chunk = lambda j, c: vts[j][:, c * kc:(c + 1) * kc]

    def q_columns(j):
        if diff:
            qt = q_ref[j].astype(F32).T
            row = lax.broadcasted_iota(jnp.int32, qt.shape, 0)
            qt = jnp.concatenate([jnp.where(row < D_A, qt, 0.0), jnp.where(row >= D_A, qt, 0.0)], axis=1)
        else:
            qt = jnp.concatenate([q_ref[2 * j].astype(F32).T, q_ref[2 * j + 1].astype(F32).T], axis=1)
        return qt.astype(BF16)

    qts = [q_columns(j) for j in range(hb)]

    def scores(item):
        j, c = item
        return jnp.dot(k_chunk(j, c), qts[j], preferred_element_type=F32)

    if diff:
        lam = (jnp.exp(jnp.sum(lq1[...] * lk1[...], axis=-1, keepdims=True))
               - jnp.exp(jnp.sum(lq2[...] * lk2[...], axis=-1, keepdims=True)) + lam_init)

    def finish(j, l, acc):
        inv = 1.0 / l
        if diff:
            o = acc[:, :tq] * inv[:, :tq] - acc[:, tq:] * (lam * inv[:, tq:])
            o = _rms(o.T, g_ref[...]) * (1.0 - lam_init)
            o_ref[j] = o.astype(BF16)
        else:
            o = acc * inv
            o_ref[2 * j] = o[:, :tq].T.astype(BF16)
            o_ref[2 * j + 1] = o[:, tq:].T.astype(BF16)

    n_chunks = n_keys // kc
    items = [(j, c) for j in range(hb) for c in range(n_chunks)]
    m = l = acc = None
    s_next = scores(items[0])
    for idx, (j, c) in enumerate(items):
        s = s_next
        if idx + 1 < len(items):
            s_next = scores(items[idx + 1])
        m_c = jnp.max(s, axis=0, keepdims=True)
        m_new = m_c if c == 0 else jnp.maximum(m, m_c)
        e = jnp.exp2(s - m_new)
        l_c = jnp.sum(e, axis=0, keepdims=True)
        pv = jnp.dot(vt_chunk(j, c), e.astype(BF16), preferred_element_type=F32)
        if c == 0:
            l, acc = l_c, pv
        else:
            alpha = jnp.exp2(m - m_new)
            l = alpha * l + l_c
            acc = alpha * acc + pv
        m = m_new
        if c == n_chunks - 1:
            finish(j, l, acc)


def _attention(q, k_new, v_new, *, diff, n_batch, tok_per_batch, tq, layer, lam_init=0.0,
               lam_params=None, subln_g=None, cache_k=None, cache_v=None):
    n_kv = k_new.shape[0]
    n = q.shape[1]
    nq = tok_per_batch // tq
    n_cache = 0 if cache_k is None else cache_k.shape[2]
    n_keys = n_cache + tok_per_batch
    kc = n_keys if n_keys <= KEY_CHUNK else math.gcd(n_keys, KEY_CHUNK)
    hb = HEADS_PER_STEP_CACHED if n_cache else n_kv

    in_specs, args = [], []
    if diff:
        in_specs += [pl.BlockSpec((1, D_A), lambda b, h, i: (0, 0))] * 4
        args += list(lam_params)
        in_specs.append(pl.BlockSpec((1, HEAD_W), lambda b, h, i: (0, 0)))
        args.append(subln_g)
    q_block = (hb if diff else 2 * hb, tq, HEAD_W)
    in_specs += [
        pl.BlockSpec(q_block, lambda b, h, i: (h, b * nq + i, 0)),
        pl.BlockSpec((hb, tok_per_batch, HEAD_W), lambda b, h, i: (h, b, 0)),
        pl.BlockSpec((hb, tok_per_batch, HEAD_W), lambda b, h, i: (h, b, 0)),
    ]
    args += [q, k_new, v_new]
    scratch = []
    if n_cache:
        cspec = pl.BlockSpec((None, None, n_cache, n_kv, HEAD_W), lambda b, h, i: (b, layer, 0, 0, 0))
        in_specs += [cspec, cspec]
        args += [cache_k, cache_v]
        scratch = [pltpu.VMEM((hb, n_keys, HEAD_W), BF16), pltpu.VMEM((hb, HEAD_W, n_keys), BF16)]
    return pl.pallas_call(
        functools.partial(_attn_kernel, diff=diff, n_cache=n_cache, n_new=tok_per_batch, tq=tq, kc=kc,
                          hb=hb, lam_init=lam_init),
        grid=(n_batch, n_kv // hb, nq),
        in_specs=in_specs,
        out_specs=pl.BlockSpec(q_block, lambda b, h, i: (h, b * nq + i, 0)),
        out_shape=jax.ShapeDtypeStruct((H_A, n, HEAD_W), BF16),
        scratch_shapes=scratch,
        compiler_params=_params(3),
        name=("attn_diff" if diff else "attn_gqa") + ("_cache" if n_cache else ""),
    )(*args)


def _layer_norm(y, g, b):
    mu = jnp.mean(y, axis=-1, keepdims=True)
    yc = y - mu
    var = jnp.mean(yc * yc, axis=-1, keepdims=True)
    return yc * lax.rsqrt(var + LN_EPS) * g + b


def _route(h2, wr_ref, rb_ref):
    logits = jnp.dot(h2, wr_ref[...], preferred_element_type=F32)
    scores = jax.nn.sigmoid(logits)
    biased = scores + rb_ref[...]
    lane = lax.broadcasted_iota(jnp.int32, biased.shape, 1)
    valid = lane < N_EXPERTS
    pos = lane & (EXPERTS_PER_GROUP - 1)

    def neighbour(kk):
        fwd = pltpu.roll(biased, LANES - kk, 1)
        bwd = pltpu.roll(biased, EXPERTS_PER_GROUP - kk, 1)
        wrapped = pos + kk >= EXPERTS_PER_GROUP
        return jnp.where(wrapped, bwd, fwd), wrapped

    (v1, w1), (v2, w2), (v3, w3) = neighbour(1), neighbour(2), neighbour(3)
    rank = jnp.zeros(biased.shape, jnp.int32)
    for v, wrapped in ((v1, w1), (v2, w2), (v3, w3)):
        ge = jnp.where(v >= biased, 1, 0)
        gt = jnp.where(v > biased, 1, 0)
        rank = rank + jnp.where(wrapped, ge, gt)
    hi_a, lo_a = jnp.maximum(biased, v1), jnp.minimum(biased, v1)
    hi_b, lo_b = jnp.maximum(v2, v3), jnp.minimum(v2, v3)
    top1 = jnp.maximum(hi_a, hi_b)
    top2 = jnp.maximum(jnp.minimum(hi_a, hi_b), jnp.maximum(lo_a, lo_b))
    grp_score = jnp.where(valid, top1 + top2, -jnp.inf)
    best = jnp.max(grp_score, axis=-1, keepdims=True)
    grp = (lane // EXPERTS_PER_GROUP).astype(F32)
    first = jnp.min(jnp.where(grp_score == best, grp, float(LANES)), axis=-1, keepdims=True)
    chosen = jnp.where(grp == first, rank, EXPERTS_PER_GROUP) < 2
    w_raw = jnp.where(chosen, scores, 0.0)
    return w_raw / jnp.sum(w_raw, axis=-1, keepdims=True)


def _post_kernel(x_ref, mod_ref, oa_ref, ob_ref, wga_ref, wgb_ref, wa_ref, wb_ref, wo_ref, lg_ref, lb_ref,
                 wr_ref, rb_ref, x1_ref, h2_ref, comb_ref):
    sh1 = mod_ref[:, 0:D_MODEL]
    sc1 = mod_ref[:, D_MODEL:2 * D_MODEL]
    g1 = mod_ref[:, 2 * D_MODEL:3 * D_MODEL]
    sh2 = mod_ref[:, 3 * D_MODEL:4 * D_MODEL]
    sc2 = mod_ref[:, 4 * D_MODEL:5 * D_MODEL]
    tm = x_ref.shape[0]
    rb = min(tm, POST_ROWS)
    blocks = [slice(r, r + rb) for r in range(0, tm, rb)]

    def branch_dots(rows):
        h = (x_ref[rows, :] * (1.0 + sc1) + sh1).astype(BF16)
        oa = jnp.concatenate([oa_ref[i, rows, :] for i in range(H_A)], axis=1)
        ob = jnp.concatenate([ob_ref[i, rows, :] for i in range(H_B)], axis=1)
        return (jnp.dot(h, wga_ref[...], preferred_element_type=F32),
                jnp.dot(oa, wa_ref[...], preferred_element_type=F32),
                jnp.dot(h, wgb_ref[...], preferred_element_type=F32),
                jnp.dot(ob, wb_ref[...], preferred_element_type=F32))

    def mix_dot(d):
        ga, ba, gb, bb = d
        merged = jax.nn.sigmoid(ga) * ba + jax.nn.sigmoid(gb) * bb
        return jnp.dot(merged.astype(BF16), wo_ref[...], preferred_element_type=F32)

    def norm_and_route(rows, mix):
        x1 = _layer_norm(ALPHA * x_ref[rows, :] + g1 * mix, lg_ref[...], lb_ref[...])
        x1_ref[rows, :] = x1
        h2 = (x1 * (1.0 + sc2) + sh2).astype(BF16)
        h2_ref[rows, :] = h2
        comb_ref[rows, :] = _route(h2, wr_ref, rb_ref)

    dots = [branch_dots(rows) for rows in blocks]
    mixes = [mix_dot(d) for d in dots]
    for rows, mix in zip(blocks, mixes):
        norm_and_route(rows, mix)


def _post(x, mod_rows, oa, ob, w_in_bf, wa_bf, wb_bf, wo_bf, ln_g, ln_b, wr_pad, rb_pad, layer, *,
          tok_per_batch, latent):
    n = x.shape[0]
    tm = min(TM_POST, tok_per_batch if latent else n)
    tiles_per_batch = max(tok_per_batch // tm, 1)
    const2 = lambda i: (0, 0)
    once = dict(pipeline_mode=pl.Buffered(1))
    wspec = pl.BlockSpec((None, D_MODEL, D_MODEL), lambda i: (layer, 0, 0), **once)
    return pl.pallas_call(
        _post_kernel,
        grid=(n // tm,),
        in_specs=[
            pl.BlockSpec((tm, D_MODEL), lambda i: (i, 0)),
            pl.BlockSpec((None, 1, 6 * D_MODEL), lambda i: (1 + i // tiles_per_batch if latent else 0, 0, 0)),
            pl.BlockSpec((H_A, tm, HEAD_W), lambda i: (0, i, 0)),
            pl.BlockSpec((H_B, tm, HEAD_W), lambda i: (0, i, 0)),
            pl.BlockSpec((None, D_MODEL, D_MODEL), lambda i: (layer, 0, W_QKV // D_MODEL), **once),
            pl.BlockSpec((None, D_MODEL, D_MODEL), lambda i: (layer, 0, W_QKV // D_MODEL + 1), **once),
            wspec, wspec, wspec,
            pl.BlockSpec((1, D_MODEL), const2),
            pl.BlockSpec((1, D_MODEL), const2),
            pl.BlockSpec((D_MODEL, LANES), const2),
            pl.BlockSpec((1, LANES), const2),
        ],
        out_specs=[
            pl.BlockSpec((tm, D_MODEL), lambda i: (i, 0)),
            pl.BlockSpec((tm, D_MODEL), lambda i: (i, 0)),
            pl.BlockSpec((tm, LANES), lambda i: (i, 0)),
        ],
        out_shape=[
            jax.ShapeDtypeStruct((n, D_MODEL), F32),
            jax.ShapeDtypeStruct((n, D_MODEL), BF16),
            jax.ShapeDtypeStruct((n, LANES), F32),
        ],
        compiler_params=_params(1),
        name="post_attn",
    )(x, mod_rows, oa, ob, w_in_bf, w_in_bf, wa_bf, wb_bf, wo_bf, ln_g, ln_b, wr_pad, rb_pad)


def _one_hot(idx, a, b):
    return jnp.where(idx == a, 1.0, jnp.where(idx == b, 1.0, 0.0)).astype(BF16)


def _moe_kernel(x1_ref, h2_ref, comb_ref, mod_ref, tri_ref, wg_ref, wu_ref, wd_ref, lg_ref, lb_ref, o_ref,
                hs_ref, cs_ref, ys_ref, pos_ref, meta_ref, *, sub, n_rows, perm, unperm):
    i, step = pl.program_id(0), pl.program_id(1)
    n_sub, n_alloc = hs_ref.shape[0], hs_ref.shape[1]
    tail = n_alloc - n_rows

    @pl.when(step == 0)
    def _():
        @pl.when(i == 0)
        def _():
            for s in range(n_sub):
                hs_ref[s, n_rows:n_alloc, :] = jnp.zeros((tail, D_MODEL), BF16)
                cs_ref[s, n_rows:n_alloc, :] = jnp.zeros((tail, LANES), F32)
            ys_ref[...] = jnp.zeros_like(ys_ref)

        for s in range(n_sub):
            tok = slice(s * sub, (s + 1) * sub)
            comb = comb_ref[tok, :]
            sel = comb > 0.0
            lane = lax.broadcasted_iota(jnp.int32, comb.shape, 1)
            ones = jnp.where(sel, 1.0, 0.0).astype(BF16)
            cum = jnp.dot(tri_ref[...], ones, preferred_element_type=F32)
            cnt = jnp.broadcast_to(cum[sub - 1:sub, :], (8, LANES)).astype(jnp.int32)
            padded = (cnt + (SORT_ALIGN - 1)) & -SORT_ALIGN
            lane8 = lax.broadcasted_iota(jnp.int32, padded.shape, 1)
            incl = padded
            for d in (1, 2, 4, 8):
                incl = incl + jnp.where(lane8 >= d, pltpu.roll(incl, d, 1), 0)
            off = incl - padded
            for k in range(N_EXPERTS):
                meta_ref[2 * s, k] = off[0, k]
                meta_ref[2 * s + 1, k] = cnt[0, k]

            pos = off[0:1, :].astype(F32) + cum - 1.0
            p_lo = jnp.min(jnp.where(sel, pos, float(n_alloc)), axis=1, keepdims=True)
            p_hi = jnp.max(jnp.where(sel, pos, -1.0), axis=1, keepdims=True)
            pos_cols = jnp.where(lane == 0, p_lo, jnp.where(lane == 1, p_hi, 0.0))
            pos_ref[tok, :] = pos_cols
            pos_rows = pos_cols.T
            r_lo, r_hi = pos_rows[0:1, :], pos_rows[1:2, :]

            c0 = comb.astype(BF16).astype(F32)
            c1 = (comb - c0).astype(BF16).astype(F32)
            c2 = (comb - c0 - c1).astype(BF16).astype(F32)
            pieces = (c0 + pltpu.roll(c1, N_EXPERTS, 1) + pltpu.roll(c2, 2 * N_EXPERTS, 1)).astype(BF16)
            h2 = h2_ref[tok, :]
            for rc in range(n_rows // perm):
                lo = rc * perm
                ridx = (lax.broadcasted_iota(jnp.int32, (perm, sub), 0) + lo).astype(F32)
                p = _one_hot(ridx, r_lo, r_hi)
                hs_ref[s, lo:lo + perm, :] = jnp.dot(p, h2, preferred_element_type=F32).astype(BF16)
                c3 = jnp.dot(p, pieces, preferred_element_type=F32)
                cs_ref[s, lo:lo + perm, :] = (c3 + pltpu.roll(c3, LANES - N_EXPERTS, 1)
                                              + pltpu.roll(c3, LANES - 2 * N_EXPERTS, 1))

    lane_c = lax.broadcasted_iota(jnp.int32, (n_sub * MOE_CHUNK, LANES), 1)
    for k in range(EXPERTS_PER_STEP):
        e = step * EXPERTS_PER_STEP + k
        offs = [meta_ref[2 * s, e] for s in range(n_sub)]
        n_chunks = functools.reduce(
            jnp.maximum, [(meta_ref[2 * s + 1, e] + (MOE_CHUNK - 1)) // MOE_CHUNK for s in range(n_sub)])

        def chunk(j, carry, k=k, e=e, offs=offs):
            starts = [pl.multiple_of(jnp.minimum(off + j * MOE_CHUNK, n_rows), SORT_ALIGN) for off in offs]
            rows = jnp.concatenate([hs_ref[s, pl.ds(st, MOE_CHUNK), :] for s, st in enumerate(starts)], axis=0)
            cw = jnp.concatenate([cs_ref[s, pl.ds(st, MOE_CHUNK), :] for s, st in enumerate(starts)], axis=0)
            w = jnp.sum(jnp.where(lane_c == e, cw, 0.0), axis=1, keepdims=True)
            g = jnp.dot(rows, wg_ref[k], preferred_element_type=F32)
            u = jnp.dot(rows, wu_ref[k], preferred_element_type=F32)
            act = (g * jax.nn.sigmoid(g)) * u * w
            y = jnp.dot(act.astype(BF16), wd_ref[k], preferred_element_type=F32).astype(BF16)
            for s, st in enumerate(starts):
                ys_ref[s, pl.ds(st, MOE_CHUNK), :] = y[s * MOE_CHUNK:(s + 1) * MOE_CHUNK, :]
            return carry

        lax.fori_loop(0, n_chunks, chunk, 0)

    @pl.when(step == N_EXPERTS // EXPERTS_PER_STEP - 1)
    def _():
        g2 = mod_ref[:, 5 * D_MODEL:6 * D_MODEL]
        ys = []
        for s in range(n_sub):
            tok = slice(s * sub, (s + 1) * sub)
            p_lo, p_hi = pos_ref[tok, 0:1], pos_ref[tok, 1:2]
            y = None
            for cc in range(n_rows // unperm):
                lo = cc * unperm
                cidx = (lax.broadcasted_iota(jnp.int32, (sub, unperm), 1) + lo).astype(F32)
                q = _one_hot(cidx, p_lo, p_hi)
                part = jnp.dot(q, ys_ref[s, lo:lo + unperm, :], preferred_element_type=F32)
                y = part if y is None else y + part
            ys.append(y)
        for s, y in enumerate(ys):
            tok = slice(s * sub, (s + 1) * sub)
            o_ref[tok, :] = _layer_norm(ALPHA * x1_ref[tok, :] + g2 * y, lg_ref[...], lb_ref[...])


def _moe(x1, h2, comb, mod_rows, w_gate, w_up, w_down, ln_g, ln_b, layer, *, tok_per_batch, latent):
    n = x1.shape[0]
    tm = min(TM_MOE, tok_per_batch if latent else n)
    tiles_per_batch = max(tok_per_batch // tm, 1)
    sub = min(MOE_SUB, tm)
    n_sub = tm // sub
    n_rows = TOP_K * sub + N_EXPERTS * SORT_ALIGN
    perm = math.gcd(n_rows, PERM_CHUNK)
    unperm = math.gcd(n_rows, UNPERM_CHUNK)
    n_alloc = n_rows + MOE_CHUNK
    tri = jnp.tril(jnp.ones((sub, sub), BF16))
    const2 = lambda i, e: (0, 0)
    once = dict(pipeline_mode=pl.Buffered(1))
    return pl.pallas_call(
        functools.partial(_moe_kernel, sub=sub, n_rows=n_rows, perm=perm, unperm=unperm),
        grid=(n // tm, N_EXPERTS // EXPERTS_PER_STEP),
        in_specs=[
            pl.BlockSpec((tm, D_MODEL), lambda i, e: (i, 0), **once),
            pl.BlockSpec((tm, D_MODEL), lambda i, e: (i, 0), **once),
            pl.BlockSpec((tm, LANES), lambda i, e: (i, 0), **once),
            pl.BlockSpec((None, 1, 6 * D_MODEL),
                         lambda i, e: (1 + i // tiles_per_batch if latent else 0, 0, 0)),
            pl.BlockSpec((sub, sub), const2, **once),
            pl.BlockSpec((None, EXPERTS_PER_STEP, D_MODEL, D_EXPERT), lambda i, e: (layer, e, 0, 0)),
            pl.BlockSpec((None, EXPERTS_PER_STEP, D_MODEL, D_EXPERT), lambda i, e: (layer, e, 0, 0)),
            pl.BlockSpec((None, EXPERTS_PER_STEP, D_EXPERT, D_MODEL), lambda i, e: (layer, e, 0, 0)),
            pl.BlockSpec((1, D_MODEL), const2),
            pl.BlockSpec((1, D_MODEL), const2),
        ],
        out_specs=pl.BlockSpec((tm, D_MODEL), lambda i, e: (i, 0)),
        out_shape=jax.ShapeDtypeStruct((n, D_MODEL), F32),
        scratch_shapes=[
            pltpu.VMEM((n_sub, n_alloc, D_MODEL), BF16),
            pltpu.VMEM((n_sub, n_alloc, LANES), F32),
            pltpu.VMEM((n_sub, n_alloc, D_MODEL), BF16),
            pltpu.VMEM((tm, LANES), F32),
            pltpu.SMEM((2 * n_sub, N_EXPERTS), jnp.int32),
        ],
        compiler_params=_params(2),
        name="moe_ln2",
    )(x1, h2, comb, mod_rows, tri, w_gate, w_up, w_down, ln_g, ln_b)


def _lambda_init(layer):
    return 0.8 - 0.6 * math.exp(-0.3 * layer)


def kernel(x_prompt, x_sample, cache_a_k, cache_a_v, cache_b_k, cache_b_v, c, c_ctx, w_mod, b_mod, w_in,
           lam_q1, lam_k1, lam_q2, lam_k2, subln_g, qn_g, kn_g, w_br_a, w_br_b, w_out, ln1_g, ln1_b,
           ln2_g, ln2_b, w_router, router_bias, w_gate, w_up, w_down):
    batch, seq, _ = x_prompt.shape
    dec_batch, dec_seq, _ = x_sample.shape

    cond8 = jnp.zeros((8, D_MODEL), F32).at[0].set(c_ctx).at[1:1 + dec_batch].set(c)
    mod = _modulation(cond8, w_mod, b_mod).reshape(DEPTH, 8, 1, 6 * D_MODEL)

    w_in_bf = w_in.astype(BF16)
    wa_bf, wb_bf, wo_bf = w_br_a.astype(BF16), w_br_b.astype(BF16), w_out.astype(BF16)
    wg_bf, wu_bf, wd_bf = w_gate.astype(BF16), w_up.astype(BF16), w_down.astype(BF16)
    wr_pad = jnp.zeros((D_MODEL, LANES), BF16).at[:, :N_EXPERTS].set(w_router.astype(BF16))
    rb_pad = jnp.zeros((1, LANES), F32).at[0, :N_EXPERTS].set(router_bias)
    rope_tabs = _rope_tables(dec_seq, D_A) + _rope_tables(dec_seq, D_B)
    row = lambda a, l: a[l].reshape(1, -1)
    cak, cav, cbk, cbv = cache_a_k, cache_a_v, cache_b_k, cache_b_v

    def layer_step(x, l, state_bufs, latent):
        tok = dec_seq if latent else seq
        nb = dec_batch if latent else batch
        outs = _qkv(x, mod[l], w_in_bf, row(qn_g, l), row(kn_g, l), l, tok_per_batch=tok,
                    rope_tabs=rope_tabs if latent else None, state_bufs=state_bufs)
        qa, ka, va, qb, kb, vb = outs[:6]
        tq = min(TQ_ATTN, tok)
        oa = _attention(qa, ka, va, diff=True, n_batch=nb, tok_per_batch=tok, tq=tq, layer=l,
                        lam_init=_lambda_init(l), subln_g=row(subln_g, l),
                        lam_params=tuple(row(a, l) for a in (lam_q1, lam_k1, lam_q2, lam_k2)),
                        cache_k=cak if latent else None, cache_v=cav if latent else None)
        ob = _attention(qb, kb, vb, diff=False, n_batch=nb, tok_per_batch=tok, tq=tq, layer=l,
                        cache_k=cbk if latent else None, cache_v=cbv if latent else None)
        x1, h2, comb = _post(x, mod[l], oa, ob, w_in_bf, wa_bf, wb_bf, wo_bf, row(ln1_g, l), row(ln1_b, l),
                             wr_pad, rb_pad, l, tok_per_batch=tok, latent=latent)
        x2 = _moe(x1, h2, comb, mod[l], wg_bf, wu_bf, wd_bf, row(ln2_g, l), row(ln2_b, l), l,
                  tok_per_batch=tok, latent=latent)
        return x2, (outs[6:] if state_bufs is not None else None)

    wa_state = H_A * HEAD_W
    wb_state = G_B * HEAD_W
    state = tuple(jnp.zeros((batch, DEPTH, seq, w), F32) for w in (wa_state, wa_state, wb_state, wb_state))
    y = x_prompt.reshape(batch * seq, D_MODEL)
    for l in range(DEPTH):
        y, state = layer_step(y, l, state, latent=False)
    y_prompt = y.reshape(batch, seq, D_MODEL)

    y = x_sample.reshape(dec_batch * dec_seq, D_MODEL)
    for l in range(DEPTH):
        y, _ = layer_step(y, l, None, latent=True)
    y_sample = y.reshape(dec_batch, dec_seq, D_MODEL)

    sak, sav, sbk, sbv = state
    return (y_prompt, y_sample,
            sak.reshape(batch, DEPTH, seq, H_A, 2 * D_A), sav.reshape(batch, DEPTH, seq, H_A, 2 * D_A),
            sbk.reshape(batch, DEPTH, seq, G_B, D_B), sbv.reshape(batch, DEPTH, seq, G_B, D_B))
```
